```python
import math
import jax, jax.numpy as jnp
from jax import lax
import numpy as np

D_MODEL = 1024
BATCH = 16
SEQ = 2048
DEPTH = 2

N_DIFF_HEADS = 4
DIFF_QK_DIM = 64
DIFF_V_DIM = 2 * DIFF_QK_DIM
ATTN_WIDTH = N_DIFF_HEADS * DIFF_V_DIM
Q_BLOCK = 128

SSD_HEAD_DIM = 64
SSD_WIDTH = D_MODEL // 2
SSD_HEADS = SSD_WIDTH // SSD_HEAD_DIM
SSD_GROUPS = 2
SSD_HEADS_PER_GROUP = SSD_HEADS // SSD_GROUPS
SSD_STATE = 64
SSD_CONV = 5
SSD_CHUNK = 128
SSD_CONV_CH = SSD_WIDTH + 2 * SSD_GROUPS * SSD_STATE
DT_MIN = 1e-3
DT_MAX = 1e-1

Q_COLS = N_DIFF_HEADS * 2 * DIFF_QK_DIM
K_COLS = Q_COLS
V_COLS = ATTN_WIDTH
Z_COLS = SSD_WIDTH
DT_COLS = 2 * SSD_HEADS
IN_COLS = Q_COLS + K_COLS + V_COLS + Z_COLS + SSD_CONV_CH + DT_COLS
SPLIT_POINTS = [Q_COLS, Q_COLS + K_COLS, Q_COLS + K_COLS + V_COLS,
                Q_COLS + K_COLS + V_COLS + Z_COLS,
                Q_COLS + K_COLS + V_COLS + Z_COLS + SSD_CONV_CH]
MIX_WIDTH = ATTN_WIDTH + SSD_WIDTH

FOURIER_GROUPS = 4
FOURIER_GROUP_DIM = D_MODEL // FOURIER_GROUPS

MOE_GROUPS = 4
MOE_EXPERTS_PER_GROUP = 8
MOE_EXPERTS = MOE_GROUPS * MOE_EXPERTS_PER_GROUP
MOE_TOP_K = 2
MOE_HIDDEN = 256
MOE_BLOCK = 128

RMS_EPS = 1e-6
N_EVEN = (DEPTH + 1) // 2
N_ODD = DEPTH // 2

kernel_name = 'hybrid_diffattn_ssd_fnet_hmoe_encoder'


def rmsnorm(x, w):
    xf = x.astype(jnp.float32)
    y = xf * lax.rsqrt(jnp.mean(xf * xf, axis=-1, keepdims=True) + RMS_EPS)
    return (y * w.astype(jnp.float32)).astype(x.dtype)


def alibi_slopes(n):
    return jnp.asarray(np.array([2.0 ** (-8.0 * (h + 1) / n) for h in range(n)], dtype=np.float32))


def diff_attention(q, k, v, lam, subln_w, lambda_init):
    b, s = q.shape[0], q.shape[1]
    nb = s // Q_BLOCK
    scale = DIFF_QK_DIM ** -0.5
    slopes = alibi_slopes(N_DIFF_HEADS)
    kf = k.astype(jnp.float32)
    vf = v.astype(jnp.float32)
    key_pos = jnp.arange(s, dtype=jnp.float32)
    qb = q.astype(jnp.float32).reshape(b, nb, Q_BLOCK, N_DIFF_HEADS, 2, DIFF_QK_DIM).transpose(1, 0, 2, 3, 4, 5)
    starts = jnp.arange(nb, dtype=jnp.int32) * Q_BLOCK

    def block(args):
        qi, start = args
        scores = jnp.einsum('bqhcd,bkhcd->bhcqk', qi, kf) * scale
        q_pos = (start + jnp.arange(Q_BLOCK, dtype=jnp.int32)).astype(jnp.float32)
        dist = jnp.abs(q_pos[:, None] - key_pos[None, :])
        scores = scores - slopes[None, :, None, None, None] * dist
        probs = jax.nn.softmax(scores, axis=-1)
        weights = probs[:, :, 0] - lam * probs[:, :, 1]
        return jnp.einsum('bhqk,bkhd->bqhd', weights, vf)

    o = lax.map(block, (qb, starts))
    o = o.transpose(1, 0, 2, 3, 4).reshape(b, s, N_DIFF_HEADS, DIFF_V_DIM)
    o = rmsnorm(o, subln_w) * (1.0 - lambda_init)
    return o.reshape(b, s, ATTN_WIDTH)


def segsum(a):
    t = a.shape[-1]
    rep = jnp.broadcast_to(a[..., :, None], a.shape + (t,))
    strict = jnp.tril(jnp.ones((t, t), dtype=bool), -1)
    rep = jnp.where(strict, rep, 0.0)
    ss = jnp.cumsum(rep, axis=-2)
    return jnp.where(jnp.tril(jnp.ones((t, t), dtype=bool)), ss, -jnp.inf)


def ssd_scan(xdt, dta, bh, ch):
    b, s, h, p = xdt.shape
    n = bh.shape[-1]
    c = s // SSD_CHUNK
    xc = xdt.reshape(b, c, SSD_CHUNK, h, p)
    bc = bh.reshape(b, c, SSD_CHUNK, h, n)
    cc = ch.reshape(b, c, SSD_CHUNK, h, n)
    ac = dta.reshape(b, c, SSD_CHUNK, h).transpose(0, 3, 1, 2)
    a_cs = jnp.cumsum(ac, axis=-1)
    l_mat = jnp.exp(segsum(ac))
    y_diag = jnp.einsum('bclhn,bcshn,bhcls,bcshp->bclhp', cc, bc, l_mat, xc)
    decay_states = jnp.exp(a_cs[..., -1:] - a_cs)
    states = jnp.einsum('bclhn,bhcl,bclhp->bchpn', bc, decay_states, xc)
    states = jnp.concatenate([jnp.zeros_like(states[:, :1]), states], axis=1)
    chunk_decay = jnp.exp(segsum(jnp.pad(a_cs[..., -1], ((0, 0), (0, 0), (1, 0)))))
    states = jnp.einsum('bhzc,bchpn->bzhpn', chunk_decay, states)[:, :-1]
    y_off = jnp.einsum('bclhn,bchpn,bhcl->bclhp', cc, states, jnp.exp(a_cs))
    return (y_diag + y_off).reshape(b, s, h, p)


def centred_dwconv(x, w, bias):
    k = w.shape[0]
    y = lax.conv_general_dilated(x, w[:, None, :], window_strides=(1,), padding=[(k // 2, k // 2)],
                                 dimension_numbers=('NWC', 'WIO', 'NWC'), feature_group_count=x.shape[-1])
    return y + bias


def ssd_mixer(z, xbc, dt_raw, conv_w, conv_b, dt_bias, a_log, d_skip, norm_w):
    b, s, _ = xbc.shape
    f32 = jnp.float32
    xbc = jax.nn.silu(centred_dwconv(xbc.astype(f32), conv_w.astype(f32), conv_b.astype(f32)))
    xs = xbc[..., :SSD_WIDTH].reshape(b, s, SSD_HEADS, SSD_HEAD_DIM)
    bm = xbc[..., SSD_WIDTH:SSD_WIDTH + SSD_GROUPS * SSD_STATE].reshape(b, s, SSD_GROUPS, SSD_STATE)
    cm = xbc[..., SSD_WIDTH + SSD_GROUPS * SSD_STATE:].reshape(b, s, SSD_GROUPS, SSD_STATE)
    bh = jnp.repeat(bm, SSD_HEADS_PER_GROUP, axis=2)
    ch = jnp.repeat(cm, SSD_HEADS_PER_GROUP, axis=2)
    dt = jax.nn.softplus(dt_raw.astype(f32).reshape(b, s, 2, SSD_HEADS) + dt_bias.astype(f32))
    a = -jnp.exp(a_log.astype(f32))
    dt_f, dt_b = dt[:, :, 0], dt[:, :, 1]
    y_f = ssd_scan(xs * dt_f[..., None], dt_f * a[0], bh, ch)
    flip = lambda t: jnp.flip(t, axis=1)
    y_b = flip(ssd_scan(flip(xs * dt_b[..., None]), flip(dt_b * a[1]), flip(bh), flip(ch)))
    y = y_f + y_b + d_skip.astype(f32)[:, None] * xs
    y = y.reshape(b, s, SSD_WIDTH) * jax.nn.silu(z.astype(f32))
    return rmsnorm(y, norm_w)


def diff_ssd_mixer(h, w_in, q_norm, k_norm, lq1, lk1, lq2, lk2, subln, conv_w, conv_b,
                   dt_bias, a_log, d_skip, ssd_norm, w_out, lambda_init):
    b, s, _ = h.shape
    f32 = jnp.float32
    proj = h @ w_in
    q, k, v, z, xbc, dt_raw = jnp.split(proj, SPLIT_POINTS, axis=-1)
    q = rmsnorm(q.reshape(b, s, N_DIFF_HEADS, 2, DIFF_QK_DIM), q_norm)
    k = rmsnorm(k.reshape(b, s, N_DIFF_HEADS, 2, DIFF_QK_DIM), k_norm)
    v = v.reshape(b, s, N_DIFF_HEADS, DIFF_V_DIM)
    lam = (jnp.exp(jnp.sum(lq1.astype(f32) * lk1.astype(f32)))
           - jnp.exp(jnp.sum(lq2.astype(f32) * lk2.astype(f32))) + lambda_init)
    attn = diff_attention(q, k, v, lam, subln, lambda_init)
    ssd = ssd_mixer(z, xbc, dt_raw, conv_w, conv_b, dt_bias, a_log, d_skip, ssd_norm)
    mixed = jnp.concatenate([attn.astype(h.dtype), ssd.astype(h.dtype)], axis=-1)
    return mixed @ w_out


def fourier_mixer(h, w_f):
    b, s, d = h.shape
    hg = h.astype(jnp.float32).reshape(b, s, FOURIER_GROUPS, FOURIER_GROUP_DIM)
    f = jnp.fft.fft2(hg, axes=(1, 3), norm='ortho').real
    return f.reshape(b, s, d).astype(h.dtype) @ w_f


def expert_dispatch(xf, expert_idx, gates, w1, w3, w2):
    t, d = xf.shape
    n_assign = t * MOE_TOP_K
    n_blocks = -(-n_assign // MOE_BLOCK) + MOE_EXPERTS
    p_rows = n_blocks * MOE_BLOCK
    flat_e = expert_idx.reshape(n_assign)
    flat_tok = jnp.repeat(jnp.arange(t, dtype=jnp.int32), MOE_TOP_K)
    flat_g = gates.reshape(n_assign)
    order = jnp.argsort(flat_e)
    se, st, sg = flat_e[order], flat_tok[order], flat_g[order]
    counts = jnp.bincount(flat_e, length=MOE_EXPERTS)
    padded = (counts + MOE_BLOCK - 1) // MOE_BLOCK * MOE_BLOCK
    start = jnp.cumsum(counts) - counts
    pad_end = jnp.cumsum(padded)
    pad_start = pad_end - padded
    dest = pad_start[se] + (jnp.arange(n_assign, dtype=jnp.int32) - start[se])
    tok_buf = jnp.full((p_rows,), t, dtype=jnp.int32).at[dest].set(st)
    block_start = jnp.arange(n_blocks, dtype=jnp.int32) * MOE_BLOCK
    block_expert = jnp.minimum(jnp.sum(block_start[:, None] >= pad_end[None, :], axis=1), MOE_EXPERTS - 1)
    x_pad = jnp.concatenate([xf, jnp.zeros((1, d), xf.dtype)], axis=0)
    xb = x_pad[tok_buf].reshape(n_blocks, MOE_BLOCK, d)

    def run_block(args):
        xblk, e = args
        hid = jax.nn.silu(xblk @ w1[e]) * (xblk @ w3[e])
        return hid @ w2[e]

    yb = lax.map(run_block, (xb, block_expert)).reshape(p_rows, d)
    y_assign = yb[dest] * sg[:, None].astype(yb.dtype)
    return jnp.zeros((t, d), yb.dtype).at[st].add(y_assign)


def hier_moe(h, rg_w, rg_b, re_w, re_b, w1, w3, w2):
    b, s, d = h.shape
    t = b * s
    f32 = jnp.float32
    xf = h.reshape(t, d)
    x32 = xf.astype(f32)
    g_prob = jax.nn.softmax(x32 @ rg_w.astype(f32) + rg_b.astype(f32), axis=-1)
    g_p, g_idx = lax.top_k(g_prob, 1)
    g_p, g_idx = g_p[:, 0], g_idx[:, 0]
    e_logits = jnp.einsum('td,dge->tge', x32, re_w.astype(f32)) + re_b.astype(f32)
    e_sel = e_logits[jnp.arange(t), g_idx]
    e_prob = jax.nn.softmax(e_sel, axis=-1)
    e_p, e_local = lax.top_k(e_prob, MOE_TOP_K)
    e_p = e_p / jnp.sum(e_p, axis=-1, keepdims=True)
    gates = g_p[:, None] * e_p
    expert_idx = g_idx[:, None] * MOE_EXPERTS_PER_GROUP + e_local
    y = expert_dispatch(xf, expert_idx, gates, w1, w3, w2)
    return y.reshape(b, s, d).astype(h.dtype)


def setup_inputs(seed: int = 0) -> dict:
    key = jax.random.key(seed)
    ks = jax.random.split(key, 26)
    f32 = jnp.float32
    nrm = lambda k, shape, scale: jax.random.normal(k, shape, f32) * scale
    gain = lambda k, shape: 1.0 + 0.02 * jax.random.normal(k, shape, f32)
    u = jax.random.uniform(ks[13], (N_EVEN, 2, SSD_HEADS), f32)
    dt0 = jnp.exp(u * (math.log(DT_MAX) - math.log(DT_MIN)) + math.log(DT_MIN))
    dt_bias = dt0 + jnp.log(-jnp.expm1(-dt0))
    a_log = jnp.log(jax.random.uniform(ks[14], (N_EVEN, 2, SSD_HEADS), f32, minval=1.0, maxval=16.0))
    return {
        'x': nrm(ks[0], (BATCH, SEQ, D_MODEL), 1.0),
        'norm_mix': gain(ks[1], (DEPTH, D_MODEL)),
        'norm_ffn': gain(ks[2], (DEPTH, D_MODEL)),
        'w_in': nrm(ks[3], (N_EVEN, D_MODEL, IN_COLS), D_MODEL ** -0.5),
        'q_norm': gain(ks[4], (N_EVEN, DIFF_QK_DIM)),
        'k_norm': gain(ks[5], (N_EVEN, DIFF_QK_DIM)),
        'lambda_q1': nrm(ks[6], (N_EVEN, DIFF_QK_DIM), 0.1),
        'lambda_k1': nrm(ks[7], (N_EVEN, DIFF_QK_DIM), 0.1),
        'lambda_q2': nrm(ks[8], (N_EVEN, DIFF_QK_DIM), 0.1),
        'lambda_k2': nrm(ks[9], (N_EVEN, DIFF_QK_DIM), 0.1),
        'attn_subln': gain(ks[10], (N_EVEN, DIFF_V_DIM)),
        'conv_w': nrm(ks[11], (N_EVEN, SSD_CONV, SSD_CONV_CH), SSD_CONV ** -0.5),
        'conv_b': nrm(ks[12], (N_EVEN, SSD_CONV_CH), 0.02),
        'dt_bias': dt_bias,
        'a_log': a_log,
        'd_skip': gain(ks[15], (N_EVEN, SSD_HEADS)),
        'ssd_norm': gain(ks[16], (N_EVEN, SSD_WIDTH)),
        'w_out': nrm(ks[17], (N_EVEN, MIX_WIDTH, D_MODEL), MIX_WIDTH ** -0.5),
        'w_fourier': nrm(ks[18], (N_ODD, D_MODEL, D_MODEL), D_MODEL ** -0.5),
        'router_group_w': nrm(ks[19], (DEPTH, D_MODEL, MOE_GROUPS), D_MODEL ** -0.5),
        'router_group_b': nrm(ks[20], (DEPTH, MOE_GROUPS), 0.01),
        'router_expert_w': nrm(ks[21], (DEPTH, D_MODEL, MOE_GROUPS, MOE_EXPERTS_PER_GROUP), D_MODEL ** -0.5),
        'router_expert_b': nrm(ks[22], (DEPTH, MOE_GROUPS, MOE_EXPERTS_PER_GROUP), 0.01),
        'expert_w1': nrm(ks[23], (DEPTH, MOE_EXPERTS, D_MODEL, MOE_HIDDEN), D_MODEL ** -0.5),
        'expert_w3': nrm(ks[24], (DEPTH, MOE_EXPERTS, D_MODEL, MOE_HIDDEN), D_MODEL ** -0.5),
        'expert_w2': nrm(ks[25], (DEPTH, MOE_EXPERTS, MOE_HIDDEN, D_MODEL), MOE_HIDDEN ** -0.5),
    }


def reference(x, norm_mix, norm_ffn, w_in, q_norm, k_norm, lambda_q1, lambda_k1, lambda_q2, lambda_k2,
              attn_subln, conv_w, conv_b, dt_bias, a_log, d_skip, ssd_norm, w_out, w_fourier,
              router_group_w, router_group_b, router_expert_w, router_expert_b,
              expert_w1, expert_w3, expert_w2):
    for l in range(DEPTH):
        i = l // 2
        h = rmsnorm(x, norm_mix[l])
        if l % 2 == 0:
            lambda_init = 0.8 - 0.6 * math.exp(-0.3 * l)
            x = x + diff_ssd_mixer(h, w_in[i], q_norm[i], k_norm[i], lambda_q1[i], lambda_k1[i],
                                   lambda_q2[i], lambda_k2[i], attn_subln[i], conv_w[i], conv_b[i],
                                   dt_bias[i], a_log[i], d_skip[i], ssd_norm[i], w_out[i], lambda_init)
        else:
            x = x + fourier_mixer(h, w_fourier[i])
        h = rmsnorm(x, norm_ffn[l])
        x = x + hier_moe(h, router_group_w[l], router_group_b[l], router_expert_w[l], router_expert_b[l],
                         expert_w1[l], expert_w3[l], expert_w2[l])
    return x
```

```python
import functools
import math

import jax
import jax.numpy as jnp
from jax import lax
from jax.experimental import pallas as pl
from jax.experimental.pallas import tpu as pltpu

D_MODEL = 1024
N_DIFF_HEADS = 4
DIFF_QK_DIM = 64
DIFF_V_DIM = 2 * DIFF_QK_DIM
ATTN_WIDTH = N_DIFF_HEADS * DIFF_V_DIM
SSD_HEAD_DIM = 64
SSD_WIDTH = D_MODEL // 2
SSD_HEADS = SSD_WIDTH // SSD_HEAD_DIM
SSD_GROUPS = 2
SSD_HEADS_PER_GROUP = SSD_HEADS // SSD_GROUPS
SSD_STATE = 64
SSD_CONV = 5
SSD_CHUNK = 128
SSD_BC = SSD_GROUPS * SSD_STATE
SSD_CONV_CH = SSD_WIDTH + 2 * SSD_BC
Q_COLS = N_DIFF_HEADS * 2 * DIFF_QK_DIM
DT_COLS = 2 * SSD_HEADS
FOURIER_GROUPS = 4
FOURIER_GROUP_DIM = D_MODEL // FOURIER_GROUPS
MOE_GROUPS = 4
MOE_EXPERTS_PER_GROUP = 8
MOE_EXPERTS = MOE_GROUPS * MOE_EXPERTS_PER_GROUP
MOE_TOP_K = 2
MOE_HIDDEN = 256
RMS_EPS = 1e-6

V7X_LANES = 128
V7X_VMEM_LIMIT = 48 * 1024 * 1024
TOKEN_TILE = 512
ATTN_Q_TILE = 256
DFT_ROW_TILE = 512
MOE_ROWS = 256
CONV_ROWS = 128

_F32 = jnp.float32
_BF16 = jnp.bfloat16
_HI = lax.Precision.HIGHEST
_NT = (((1,), (1,)), ((), ()))
_TN = (((0,), (0,)), ((), ()))


def _params(*sem):
    return pltpu.CompilerParams(dimension_semantics=sem, vmem_limit_bytes=V7X_VMEM_LIMIT)


def _tile(n, want):
    t = min(n, want)
    assert n % t == 0, (n, t)
    return t


def _full(shape):
    return pl.BlockSpec(shape, lambda *_: (0,) * len(shape))


def _rms_rows(x, w):
    ms = jnp.mean(x * x, axis=-1, keepdims=True)
    return x * lax.rsqrt(ms + RMS_EPS) * w


def _silu(x):
    return x * jax.nn.sigmoid(x)


def _softplus(x):
    return jnp.maximum(x, 0.0) + jnp.log1p(jnp.exp(-jnp.abs(x)))


def _in_proj_kernel(x_ref, nw_ref, wqkv_ref, wz_ref, wxbc_ref, wdt_ref, wdtt_ref, qn_ref, kn_ref,
                    q_ref, k_ref, v_ref, z_ref, xbc_ref, dt_ref, dtt_ref):
    tm = x_ref.shape[0]
    h = _rms_rows(x_ref[...], nw_ref[...]).astype(_BF16)
    qkv = jnp.dot(h, wqkv_ref[...], preferred_element_type=_F32)
    low = lax.broadcasted_iota(jnp.int32, (tm, V7X_LANES), 1) < DIFF_QK_DIM

    def qk_norm(slab, w):
        sq = slab * slab
        s_lo = jnp.sum(jnp.where(low, sq, 0.0), axis=-1, keepdims=True)
        s_hi = jnp.sum(jnp.where(low, 0.0, sq), axis=-1, keepdims=True)
        ms = jnp.where(low, s_lo, s_hi) * (1.0 / DIFF_QK_DIM)
        return slab * lax.rsqrt(ms + RMS_EPS) * w

    scale = DIFF_QK_DIM ** -0.5
    for hd in range(N_DIFF_HEADS):
        c0 = hd * DIFF_V_DIM
        q_ref[:, c0:c0 + DIFF_V_DIM] = (qk_norm(qkv[:, c0:c0 + DIFF_V_DIM], qn_ref[...]) * scale).astype(_BF16)
        k_ref[:, c0:c0 + DIFF_V_DIM] = qk_norm(qkv[:, Q_COLS + c0:Q_COLS + c0 + DIFF_V_DIM], kn_ref[...]).astype(_BF16)
    v_ref[...] = qkv[:, 2 * Q_COLS:].astype(_BF16)
    z_ref[...] = jnp.dot(h, wz_ref[...], preferred_element_type=_F32).astype(_BF16)
    xbc_ref[...] = jnp.dot(h, wxbc_ref[...], preferred_element_type=_F32)
    dt_ref[...] = jnp.dot(h, wdt_ref[...], preferred_element_type=_F32)
    dtt = lax.dot_general(wdtt_ref[...], h, _NT, preferred_element_type=_F32)
    for j in range(tm // SSD_CHUNK):
        dtt_ref[j] = dtt[:, j * SSD_CHUNK:(j + 1) * SSD_CHUNK]


def _in_proj(x2d, norm_w, w_in, q_norm, k_norm):
    t = x2d.shape[0]
    tm = _tile(t, TOKEN_TILE)
    c_qkv = 3 * Q_COLS
    c_z = c_qkv + SSD_WIDTH
    c_xbc = c_z + SSD_CONV_CH
    wb = w_in.astype(_BF16)
    wqkv, wz, wxbc, wdt = wb[:, :c_qkv], wb[:, c_qkv:c_z], wb[:, c_z:c_xbc], wb[:, c_xbc:]
    row = lambda i: (i, 0)
    outs = pl.pallas_call(
        _in_proj_kernel,
        grid=(t // tm,),
        in_specs=[pl.BlockSpec((tm, D_MODEL), row), _full((1, D_MODEL)),
                  _full((D_MODEL, c_qkv)), _full((D_MODEL, SSD_WIDTH)), _full((D_MODEL, SSD_CONV_CH)),
                  _full((D_MODEL, DT_COLS)), _full((DT_COLS, D_MODEL)),
                  _full((1, DIFF_V_DIM)), _full((1, DIFF_V_DIM))],
        out_specs=[pl.BlockSpec((tm, Q_COLS), row), pl.BlockSpec((tm, Q_COLS), row),
                   pl.BlockSpec((tm, ATTN_WIDTH), row), pl.BlockSpec((tm, SSD_WIDTH), row),
                   pl.BlockSpec((tm, SSD_CONV_CH), row), pl.BlockSpec((tm, DT_COLS), row),
                   pl.BlockSpec((tm // SSD_CHUNK, DT_COLS, SSD_CHUNK), lambda i: (i, 0, 0))],
        out_shape=[jax.ShapeDtypeStruct((t, Q_COLS), _BF16), jax.ShapeDtypeStruct((t, Q_COLS), _BF16),
                   jax.ShapeDtypeStruct((t, ATTN_WIDTH), _BF16), jax.ShapeDtypeStruct((t, SSD_WIDTH), _BF16),
                   jax.ShapeDtypeStruct((t, SSD_CONV_CH), _F32), jax.ShapeDtypeStruct((t, DT_COLS), _F32),
                   jax.ShapeDtypeStruct((t // SSD_CHUNK, DT_COLS, SSD_CHUNK), _F32)],
        compiler_params=_params("parallel"),
        name="in_proj",
    )(x2d, norm_w.reshape(1, D_MODEL), wqkv, wz, wxbc, wdt, wdt.T,
      jnp.tile(q_norm, 2).reshape(1, DIFF_V_DIM), jnp.tile(k_norm, 2).reshape(1, DIFF_V_DIM))
    return outs


def _attn_kernel(slope_ref, lq1_ref, lk1_ref, lq2_ref, lk2_ref, q_ref, k_ref, v_ref, subln_ref, o_ref,
                 *, lambda_init):
    tq = q_ref.shape[0]
    seq = k_ref.shape[0]
    hd = pl.program_id(1)
    qi = pl.program_id(2)
    lam = (jnp.exp(jnp.sum(lq1_ref[...] * lk1_ref[...], keepdims=True))
           - jnp.exp(jnp.sum(lq2_ref[...] * lk2_ref[...], keepdims=True)) + lambda_init)
    q = q_ref[...]
    k = k_ref[...]
    low = lax.broadcasted_iota(jnp.int32, (tq, V7X_LANES), 1) < DIFF_QK_DIM
    zero = jnp.zeros_like(q)
    row = qi * tq + lax.broadcasted_iota(jnp.int32, (tq, seq), 0)
    col = lax.broadcasted_iota(jnp.int32, (tq, seq), 1)
    bias = slope_ref[hd] * jnp.abs(row - col).astype(_F32)

    def branch(qc):
        s = lax.dot_general(qc, k, _NT, preferred_element_type=_F32) - bias
        p = jnp.exp(s - jnp.max(s, axis=-1, keepdims=True))
        return p, jnp.sum(p, axis=-1, keepdims=True)

    p1, l1 = branch(jnp.where(low, q, zero))
    p2, l2 = branch(jnp.where(low, zero, q))
    w = p1 * (1.0 / l1) - p2 * (lam / l2)
    o = jnp.dot(w.astype(_BF16), v_ref[...], preferred_element_type=_F32)
    o_ref[...] = (_rms_rows(o, subln_ref[...]) * (1.0 - lambda_init)).astype(_BF16)


def _diff_attention(q, k, v, lq1, lk1, lq2, lk2, subln, lambda_init, batch, seq):
    tq = _tile(seq, ATTN_Q_TILE)
    slopes = jnp.asarray([2.0 ** (-8.0 * (h + 1) / N_DIFF_HEADS) for h in range(N_DIFF_HEADS)], _F32)
    vec = lambda a: a.reshape(1, -1).astype(_F32)
    q3, k3, v3 = (a.reshape(batch, seq, ATTN_WIDTH) for a in (q, k, v))
    kv_spec = pl.BlockSpec((None, seq, DIFF_V_DIM), lambda b, h, i: (b, 0, h))
    q_spec = pl.BlockSpec((None, tq, DIFF_V_DIM), lambda b, h, i: (b, i, h))
    out = pl.pallas_call(
        functools.partial(_attn_kernel, lambda_init=lambda_init),
        grid=(batch, N_DIFF_HEADS, seq // tq),
        in_specs=[pl.BlockSpec(memory_space=pltpu.SMEM)] + [_full((1, DIFF_QK_DIM))] * 4
                 + [q_spec, kv_spec, kv_spec, _full((1, DIFF_V_DIM))],
        out_specs=q_spec,
        out_shape=jax.ShapeDtypeStruct((batch, seq, ATTN_WIDTH), _BF16),
        compiler_params=_params("parallel", "parallel", "parallel"),
        name="diff_attn",
    )(slopes, vec(lq1), vec(lk1), vec(lq2), vec(lk2), q3, k3, v3, vec(subln))
    return out.reshape(batch * seq, ATTN_WIDTH)


def _ssd_kernel(xbc_ref, z_ref, dt_ref, dtt_ref, cw_ref, cb_ref, dtb_row_ref, dtb_col_ref,
                alog_row_ref, alog_col_ref, dskip_ref, nw_ref, o_ref,
                xpad_ref, xc_ref, y_ref, state_ref):
    seq = xbc_ref.shape[0]
    n_chunks = seq // SSD_CHUNK
    L = SSD_CHUNK
    halo = 8
    pad = SSD_CONV // 2

    xpad_ref[0:halo, :] = jnp.zeros((halo, SSD_CONV_CH), _F32)
    xpad_ref[halo + seq:2 * halo + seq, :] = jnp.zeros((halo, SSD_CONV_CH), _F32)
    xpad_ref[halo:halo + seq, :] = xbc_ref[...]
    for rb in range(seq // CONV_ROWS):
        base = rb * CONV_ROWS + halo - pad
        acc = jnp.broadcast_to(cb_ref[...], (CONV_ROWS, SSD_CONV_CH))
        for j in range(SSD_CONV):
            acc = acc + cw_ref[j:j + 1, :] * xpad_ref[base + j:base + j + CONV_ROWS, :]
        xc_ref[rb * CONV_ROWS:(rb + 1) * CONV_ROWS, :] = _silu(acc).astype(_BF16)

    sub = lax.broadcasted_iota(jnp.int32, (L, L), 0)
    lane = lax.broadcasted_iota(jnp.int32, (L, L), 1)
    causal = lane <= sub
    anti = lane >= sub
    tri = causal.astype(_F32)
    trit = anti.astype(_F32)
    neg_big = jnp.float32(-1e30)
    head_of_col = lax.broadcasted_iota(jnp.int32, (DT_COLS, SSD_WIDTH), 1) // SSD_HEAD_DIM
    dtcol = lax.broadcasted_iota(jnp.int32, (DT_COLS, SSD_WIDTH), 0)
    expand_f = (dtcol == head_of_col).astype(_F32)
    expand_b = (dtcol == head_of_col + SSD_HEADS).astype(_F32)
    state_row_group = lax.broadcasted_iota(jnp.int32, (SSD_BC, SSD_WIDTH), 0) // SSD_STATE
    state_col_group = (lax.broadcasted_iota(jnp.int32, (SSD_BC, SSD_WIDTH), 1)
                       // (SSD_HEAD_DIM * SSD_HEADS_PER_GROUP))
    state_mask = state_row_group == state_col_group
    a_row = -jnp.exp(alog_row_ref[...])
    a_col = -jnp.exp(alog_col_ref[...])

    def hi_dot(a, b):
        return jnp.dot(a, b, preferred_element_type=_F32, precision=_HI)

    def load_chunk(c):
        r0 = pl.multiple_of(c * L, L)
        xs = xc_ref[pl.ds(r0, L), 0:SSD_WIDTH]
        bc = xc_ref[pl.ds(r0, L), SSD_WIDTH:SSD_WIDTH + SSD_BC]
        cc = xc_ref[pl.ds(r0, L), SSD_WIDTH + SSD_BC:SSD_CONV_CH]
        dtc = _softplus(dt_ref[pl.ds(r0, L), :] + dtb_row_ref[...])
        a_c = dtc * a_row
        acs_c = hi_dot(tri, a_c)
        return r0, xs, bc, cc, dtc, a_c, acs_c

    def state_step(state, bc, xs, weight_cols, total_row, expand):
        wexp = hi_dot(weight_cols, expand)
        xw = (xs.astype(_F32) * wexp).astype(_BF16)
        new = lax.dot_general(bc, xw, _TN, preferred_element_type=_F32)
        decay = jnp.exp(hi_dot(jnp.broadcast_to(total_row, (8, DT_COLS)), expand))[0:1, :]
        return state * decay + jnp.where(state_mask, new, 0.0)

    state_ref[...] = jnp.zeros_like(state_ref)

    def fwd_body(c, carry):
        r0, xs, bc, cc, dtc, a_c, acs_c = load_chunk(c)
        dtr = _softplus(dtt_ref[c] + dtb_col_ref[...])
        a_r = dtr * a_col
        acs_r = hi_dot(a_r, trit)
        ecs_c = acs_c - a_c
        ecs_r = acs_r - a_r
        lane_bc = lax.broadcasted_iota(jnp.int32, (L, SSD_BC), 1)
        gmat = []
        for g in range(SSD_GROUPS):
            in_g = (lane_bc >= g * SSD_STATE) & (lane_bc < (g + 1) * SSD_STATE)
            cg = jnp.where(in_g, cc, jnp.zeros_like(cc))
            gmat.append(lax.dot_general(cg, bc, _NT, preferred_element_type=_F32))
        parts = []
        for h in range(SSD_HEADS):
            hb = SSD_HEADS + h
            lf = jnp.exp(jnp.where(causal, acs_c[:, h:h + 1] - acs_r[h:h + 1, :], neg_big))
            lb = jnp.exp(jnp.where(anti, ecs_r[hb:hb + 1, :] - ecs_c[:, hb:hb + 1], neg_big))
            w = gmat[h // SSD_HEADS_PER_GROUP] * (lf * dtr[h:h + 1, :] + lb * dtr[hb:hb + 1, :])
            parts.append(jnp.dot(w.astype(_BF16), xs[:, h * SSD_HEAD_DIM:(h + 1) * SSD_HEAD_DIM],
                                 preferred_element_type=_F32))
        y_diag = jnp.concatenate(parts, axis=-1)
        state = state_ref[...]
        y_off = (jnp.dot(cc, state.astype(_BF16), preferred_element_type=_F32)
                 * hi_dot(jnp.exp(acs_c), expand_f))
        y_ref[pl.ds(r0, L), :] = y_diag + y_off
        total = acs_c[L - 1:L, :]
        state_ref[...] = state_step(state, bc, xs, jnp.exp(total - acs_c) * dtc, total, expand_f)
        return carry

    lax.fori_loop(0, n_chunks, fwd_body, 0)

    state_ref[...] = jnp.zeros_like(state_ref)

    def bwd_body(i, carry):
        c = n_chunks - 1 - i
        r0, xs, bc, cc, dtc, a_c, acs_c = load_chunk(c)
        ecs_c = acs_c - a_c
        total = acs_c[L - 1:L, :]
        state = state_ref[...]
        y_off = (jnp.dot(cc, state.astype(_BF16), preferred_element_type=_F32)
                 * hi_dot(jnp.exp(total - ecs_c), expand_b))
        y = y_ref[pl.ds(r0, L), :] + y_off + dskip_ref[...] * xs.astype(_F32)
        gated = y * _silu(z_ref[pl.ds(r0, L), :].astype(_F32))
        o_ref[pl.ds(r0, L), :] = _rms_rows(gated, nw_ref[...]).astype(_BF16)
        state_ref[...] = state_step(state, bc, xs, jnp.exp(ecs_c) * dtc, total, expand_b)
        return carry

    lax.fori_loop(0, n_chunks, bwd_body, 0)


def _ssd_mixer(z, xbc, dt, dtt, conv_w, conv_b, dt_bias, a_log, d_skip, norm_w, batch, seq):
    n_chunks = seq // SSD_CHUNK
    per_b = lambda w: pl.BlockSpec((None, seq, w), lambda b: (b, 0, 0))
    dtb = dt_bias.reshape(DT_COLS).astype(_F32)
    alog = a_log.reshape(DT_COLS).astype(_F32)
    out = pl.pallas_call(
        _ssd_kernel,
        grid=(batch,),
        in_specs=[per_b(SSD_CONV_CH), per_b(SSD_WIDTH), per_b(DT_COLS),
                  pl.BlockSpec((n_chunks, DT_COLS, SSD_CHUNK), lambda b: (b, 0, 0)),
                  _full((SSD_CONV, SSD_CONV_CH)), _full((1, SSD_CONV_CH)),
                  _full((1, DT_COLS)), _full((DT_COLS, 1)), _full((1, DT_COLS)), _full((DT_COLS, 1)),
                  _full((1, SSD_WIDTH)), _full((1, SSD_WIDTH))],
        out_specs=per_b(SSD_WIDTH),
        out_shape=jax.ShapeDtypeStruct((batch, seq, SSD_WIDTH), _BF16),
        scratch_shapes=[pltpu.VMEM((seq + 16, SSD_CONV_CH), _F32), pltpu.VMEM((seq, SSD_CONV_CH), _BF16),
                        pltpu.VMEM((seq, SSD_WIDTH), _F32), pltpu.VMEM((SSD_BC, SSD_WIDTH), _F32)],
        compiler_params=_params("parallel"),
        name="ssd_mixer",
    )(xbc.reshape(batch, seq, SSD_CONV_CH), z.reshape(batch, seq, SSD_WIDTH), dt.reshape(batch, seq, DT_COLS),
      dtt, conv_w.astype(_F32), conv_b.reshape(1, SSD_CONV_CH).astype(_F32),
      dtb.reshape(1, DT_COLS), dtb.reshape(DT_COLS, 1), alog.reshape(1, DT_COLS), alog.reshape(DT_COLS, 1),
      jnp.repeat(d_skip.astype(_F32), SSD_HEAD_DIM).reshape(1, SSD_WIDTH), norm_w.reshape(1, SSD_WIDTH))
    return out.reshape(batch * seq, SSD_WIDTH)


def _proj_router_kernel(*refs, n_act):
    x_ref = refs[0]
    acts = refs[1:1 + n_act]
    ws = refs[1 + n_act:1 + 2 * n_act]
    nw_ref, wr_ref, br_ref, x1_ref, h_ref, eidx_ref, gate_ref = refs[1 + 2 * n_act:]
    tm = x_ref.shape[0]
    x1 = x_ref[...]
    for a_ref, w_ref in zip(acts, ws):
        x1 = x1 + jnp.dot(a_ref[...], w_ref[...], preferred_element_type=_F32)
    x1_ref[...] = x1
    h = _rms_rows(x1, nw_ref[...])
    h_ref[...] = h

    logits = jnp.dot(h, wr_ref[...], preferred_element_type=_F32, precision=_HI) + br_ref[...]
    lane = lax.broadcasted_iota(jnp.int32, (tm, V7X_LANES), 1)
    lane_f = lane.astype(_F32)
    ninf = jnp.float32(-jnp.inf)
    nolane = jnp.float32(V7X_LANES)

    def argmax_first(vals):
        m = jnp.max(vals, axis=-1, keepdims=True)
        idx = jnp.min(jnp.where(vals == m, lane_f, nolane), axis=-1, keepdims=True)
        return m, idx

    gl = jnp.where(lane < MOE_GROUPS, logits, ninf)
    gmax, gidx = argmax_first(gl)
    g_p = 1.0 / jnp.sum(jnp.exp(gl - gmax), axis=-1, keepdims=True)
    first = MOE_GROUPS + MOE_EXPERTS_PER_GROUP * gidx
    el = jnp.where((lane_f >= first) & (lane_f < first + MOE_EXPERTS_PER_GROUP), logits, ninf)
    e1, i1 = argmax_first(el)
    e2, i2 = argmax_first(jnp.where(lane_f == i1, ninf, el))
    r = jnp.exp(e2 - e1)
    p1 = 1.0 / (1.0 + r)
    gate_ref[...] = jnp.where(lane == 0, g_p * p1, jnp.where(lane == 1, g_p * (r * p1), 0.0))
    eidx_ref[...] = jnp.where(lane == 0, i1 - MOE_GROUPS, jnp.where(lane == 1, i2 - MOE_GROUPS, 0.0)).astype(jnp.int32)


def _proj_router(x2d, acts, weights, norm_w, rg_w, rg_b, re_w, re_b):
    t = x2d.shape[0]
    tm = _tile(t, TOKEN_TILE)
    n_act = len(acts)
    n_log = MOE_GROUPS + MOE_EXPERTS
    wr = jnp.zeros((D_MODEL, V7X_LANES), _F32).at[:, :n_log].set(
        jnp.concatenate([rg_w.astype(_F32), re_w.reshape(D_MODEL, MOE_EXPERTS).astype(_F32)], axis=1))
    br = jnp.zeros((1, V7X_LANES), _F32).at[0, :n_log].set(
        jnp.concatenate([rg_b.astype(_F32), re_b.reshape(MOE_EXPERTS).astype(_F32)]))
    row = lambda i: (i, 0)
    x1, h, eidx, gate = pl.pallas_call(
        functools.partial(_proj_router_kernel, n_act=n_act),
        grid=(t // tm,),
        in_specs=[pl.BlockSpec((tm, D_MODEL), row)]
                 + [pl.BlockSpec((tm, a.shape[1]), row) for a in acts]
                 + [_full(w.shape) for w in weights]
                 + [_full((1, D_MODEL)), _full((D_MODEL, V7X_LANES)), _full((1, V7X_LANES))],
        out_specs=[pl.BlockSpec((tm, D_MODEL), row), pl.BlockSpec((tm, D_MODEL), row),
                   pl.BlockSpec((tm, V7X_LANES), row), pl.BlockSpec((tm, V7X_LANES), row)],
        out_shape=[jax.ShapeDtypeStruct((t, D_MODEL), _F32), jax.ShapeDtypeStruct((t, D_MODEL), _F32),
                   jax.ShapeDtypeStruct((t, V7X_LANES), jnp.int32), jax.ShapeDtypeStruct((t, V7X_LANES), _F32)],
        compiler_params=_params("parallel"),
        name="proj_router",
    )(x2d, *acts, *[w.astype(_BF16) for w in weights], norm_w.reshape(1, D_MODEL), wr, br)
    return x1, h, eidx, gate


def _expert_kernel(rows_ref, bexp_ref, bstart_ref, bvalid_ref, h_hbm, w13_ref, w2_ref, out_hbm,
                   xbuf, ybuf, sem_in, sem_out, *, n_tok):
    i = pl.program_id(0)
    n_valid = bvalid_ref[i]
    start = bstart_ref[i]

    @pl.when(i == 0)
    def _():
        xbuf[...] = jnp.zeros_like(xbuf)

    def gather_copy(r):
        row = rows_ref[start + r]
        tok = jnp.where(row >= n_tok, row - n_tok, row)
        return pltpu.make_async_copy(h_hbm.at[pl.ds(tok, 1)], xbuf.at[pl.ds(r, 1)], sem_in)

    def scatter_copy(r):
        row = rows_ref[start + r]
        return pltpu.make_async_copy(ybuf.at[pl.ds(r, 1)], out_hbm.at[pl.ds(row, 1)], sem_out)

    def for_valid(fn):
        def body(r, carry):
            fn(r)
            return carry
        lax.fori_loop(0, n_valid, body, 0)

    @pl.when(n_valid > 0)
    def _():
        for_valid(lambda r: gather_copy(r).start())
        for_valid(lambda r: gather_copy(r).wait())
        xb = xbuf[...].astype(_BF16)
        a = jnp.dot(xb, w13_ref[...], preferred_element_type=_F32)
        hid = _silu(a[:, :MOE_HIDDEN]) * a[:, MOE_HIDDEN:]
        ybuf[...] = jnp.dot(hid.astype(_BF16), w2_ref[...], preferred_element_type=_F32)
        for_valid(lambda r: scatter_copy(r).start())
        for_valid(lambda r: scatter_copy(r).wait())


def _expert_plan(eidx, n_tok):
    n_assign = n_tok * MOE_TOP_K
    n_blocks = n_assign // MOE_ROWS + MOE_EXPERTS
    flat_e = eidx.reshape(n_assign)
    order = jnp.argsort(flat_e).astype(jnp.int32)
    rows = (order % MOE_TOP_K) * n_tok + order // MOE_TOP_K
    counts = jnp.bincount(flat_e, length=MOE_EXPERTS).astype(jnp.int32)
    seg_start = jnp.cumsum(counts) - counts
    nblk = (counts + MOE_ROWS - 1) // MOE_ROWS
    blk_end = jnp.cumsum(nblk)
    blk_first = blk_end - nblk
    bid = jnp.arange(n_blocks, dtype=jnp.int32)
    used = bid < blk_end[-1]
    bexp = jnp.minimum(jnp.sum(bid[:, None] >= blk_end[None, :], axis=1), MOE_EXPERTS - 1).astype(jnp.int32)
    last_exp = jnp.max(jnp.where(counts > 0, jnp.arange(MOE_EXPERTS, dtype=jnp.int32), 0))
    bexp = jnp.where(used, bexp, last_exp)
    local = bid - blk_first[bexp]
    bstart = jnp.where(used, seg_start[bexp] + local * MOE_ROWS, 0).astype(jnp.int32)
    bvalid = jnp.where(used, jnp.clip(counts[bexp] - local * MOE_ROWS, 0, MOE_ROWS), 0).astype(jnp.int32)
    return rows.astype(jnp.int32), bexp, bstart, bvalid, n_blocks


def _experts(h, eidx, w1, w3, w2):
    n_tok = h.shape[0]
    rows, bexp, bstart, bvalid, n_blocks = _expert_plan(eidx, n_tok)
    w13 = jnp.concatenate([w1, w3], axis=-1).astype(_BF16)
    w2b = w2.astype(_BF16)
    grid_spec = pltpu.PrefetchScalarGridSpec(
        num_scalar_prefetch=4,
        grid=(n_blocks,),
        in_specs=[pl.BlockSpec(memory_space=pl.ANY),
                  pl.BlockSpec((None, D_MODEL, 2 * MOE_HIDDEN), lambda i, rows, bexp, bstart, bvalid: (bexp[i], 0, 0)),
                  pl.BlockSpec((None, MOE_HIDDEN, D_MODEL), lambda i, rows, bexp, bstart, bvalid: (bexp[i], 0, 0))],
        out_specs=pl.BlockSpec(memory_space=pl.ANY),
        scratch_shapes=[pltpu.VMEM((MOE_ROWS, D_MODEL), _F32), pltpu.VMEM((MOE_ROWS, D_MODEL), _F32),
                        pltpu.SemaphoreType.DMA(()), pltpu.SemaphoreType.DMA(())],
    )
    return pl.pallas_call(
        functools.partial(_expert_kernel, n_tok=n_tok),
        grid_spec=grid_spec,
        out_shape=jax.ShapeDtypeStruct((MOE_TOP_K * n_tok, D_MODEL), _F32),
        compiler_params=_params("arbitrary"),
        name="experts",
    )(rows, bexp, bstart, bvalid, h, w13, w2b)


def _combine(x1_ref, y0_ref, y1_ref, gate_ref):
    gate = gate_ref[...]
    return x1_ref[...] + gate[:, 0:1] * y0_ref[...] + gate[:, 1:2] * y1_ref[...]


def _combine_kernel(x1_ref, y0_ref, y1_ref, gate_ref, o_ref):
    o_ref[...] = _combine(x1_ref, y0_ref, y1_ref, gate_ref)


def _combine_dft_kernel(x1_ref, y0_ref, y1_ref, gate_ref, nw_ref, cosc_ref, sinc_ref, x_ref, hc_ref, hs_ref):
    x = _combine(x1_ref, y0_ref, y1_ref, gate_ref)
    x_ref[...] = x
    h = _rms_rows(x, nw_ref[...]).astype(_BF16)
    for g in range(FOURIER_GROUPS):
        c0 = g * FOURIER_GROUP_DIM
        hg = h[:, c0:c0 + FOURIER_GROUP_DIM]
        hc_ref[:, c0:c0 + FOURIER_GROUP_DIM] = jnp.dot(hg, cosc_ref[...], preferred_element_type=_F32).astype(_BF16)
        hs_ref[:, c0:c0 + FOURIER_GROUP_DIM] = jnp.dot(hg, sinc_ref[...], preferred_element_type=_F32).astype(_BF16)


def _moe_combine(x1, y, gate, dft=None):
    t = x1.shape[0]
    tm = _tile(t, TOKEN_TILE)
    row = lambda i: (i, 0)
    second = lambda i: (i + t // tm, 0)
    in_specs = [pl.BlockSpec((tm, D_MODEL), row), pl.BlockSpec((tm, D_MODEL), row),
                pl.BlockSpec((tm, D_MODEL), second), pl.BlockSpec((tm, V7X_LANES), row)]
    if dft is None:
        return pl.pallas_call(
            _combine_kernel, grid=(t // tm,), in_specs=in_specs,
            out_specs=pl.BlockSpec((tm, D_MODEL), row),
            out_shape=jax.ShapeDtypeStruct((t, D_MODEL), _F32),
            compiler_params=_params("parallel"), name="moe_combine",
        )(x1, y, y, gate)
    norm_w, cosc, sinc = dft
    return pl.pallas_call(
        _combine_dft_kernel, grid=(t // tm,),
        in_specs=in_specs + [_full((1, D_MODEL)), _full(cosc.shape), _full(sinc.shape)],
        out_specs=[pl.BlockSpec((tm, D_MODEL), row)] * 3,
        out_shape=[jax.ShapeDtypeStruct((t, D_MODEL), _F32), jax.ShapeDtypeStruct((t, D_MODEL), _BF16),
                   jax.ShapeDtypeStruct((t, D_MODEL), _BF16)],
        compiler_params=_params("parallel"), name="moe_combine_dft",
    )(x1, y, y, gate, norm_w.reshape(1, D_MODEL), cosc, sinc)


def _seq_dft_kernel(cos_ref, nsin_ref, hc_ref, hs_ref, o_ref):
    o_ref[...] = (jnp.dot(cos_ref[...], hc_ref[...], preferred_element_type=_F32)
                  + jnp.dot(nsin_ref[...], hs_ref[...], preferred_element_type=_F32)).astype(_BF16)


def _seq_dft(hc, hs, cos_s, nsin_s, batch, seq):
    tq = _tile(seq, DFT_ROW_TILE)
    tab = pl.BlockSpec((tq, seq), lambda b, i: (i, 0))
    per_b = pl.BlockSpec((None, seq, D_MODEL), lambda b, i: (b, 0, 0))
    out = pl.pallas_call(
        _seq_dft_kernel, grid=(batch, seq // tq),
        in_specs=[tab, tab, per_b, per_b],
        out_specs=pl.BlockSpec((None, tq, D_MODEL), lambda b, i: (b, i, 0)),
        out_shape=jax.ShapeDtypeStruct((batch, seq, D_MODEL), _BF16),
        compiler_params=_params("parallel", "parallel"), name="seq_dft",
    )(cos_s, nsin_s, hc.reshape(batch, seq, D_MODEL), hs.reshape(batch, seq, D_MODEL))
    return out.reshape(batch * seq, D_MODEL)


def _dft_tables(n, scale):
    idx = jnp.arange(n, dtype=jnp.int32)
    m = (idx[:, None] * idx[None, :]) % n
    ang = m.astype(_F32) * (2.0 * math.pi / n)
    return jnp.cos(ang) * scale, jnp.sin(ang) * scale


def kernel(x, norm_mix, norm_ffn, w_in, q_norm, k_norm, lambda_q1, lambda_k1, lambda_q2, lambda_k2,
           attn_subln, conv_w, conv_b, dt_bias, a_log, d_skip, ssd_norm, w_out, w_fourier,
           router_group_w, router_group_b, router_expert_w, router_expert_b,
           expert_w1, expert_w3, expert_w2):
    batch, seq, _ = x.shape
    depth = norm_mix.shape[0]
    xt = x.reshape(batch * seq, D_MODEL)
    pending = None
    for l in range(depth):
        i = l // 2
        if l % 2 == 0:
            if pending is not None:
                xt = _moe_combine(*pending)
            lambda_init = 0.8 - 0.6 * math.exp(-0.3 * l)
            q, k, v, z, xbc, dt, dtt = _in_proj(xt, norm_mix[l], w_in[i], q_norm[i], k_norm[i])
            attn = _diff_attention(q, k, v, lambda_q1[i], lambda_k1[i], lambda_q2[i], lambda_k2[i],
                                   attn_subln[i], lambda_init, batch, seq)
            ssd = _ssd_mixer(z, xbc, dt, dtt, conv_w[i], conv_b[i], dt_bias[i], a_log[i], d_skip[i],
                             ssd_norm[i], batch, seq)
            acts = [attn, ssd]
            weights = [w_out[i][:ATTN_WIDTH], w_out[i][ATTN_WIDTH:]]
        else:
            cos_c, sin_c = _dft_tables(FOURIER_GROUP_DIM, FOURIER_GROUP_DIM ** -0.5)
            cos_s, sin_s = _dft_tables(seq, seq ** -0.5)
            dft = (norm_mix[l], cos_c.astype(_BF16), sin_c.astype(_BF16))
            if pending is None:
                zeros = jnp.zeros((2 * batch * seq, D_MODEL), _F32)
                pending = (xt, zeros, jnp.zeros((batch * seq, V7X_LANES), _F32))
            xt, hc, hs = _moe_combine(*pending, dft=dft)
            f = _seq_dft(hc, hs, cos_s.astype(_BF16), (-sin_s).astype(_BF16), batch, seq)
            acts = [f]
            weights = [w_fourier[i]]
        x1, h, eidx, gate = _proj_router(xt, acts, weights, norm_ffn[l], router_group_w[l], router_group_b[l],
                                         router_expert_w[l], router_expert_b[l])
        y = _experts(h, eidx[:, :MOE_TOP_K], expert_w1[l], expert_w3[l], expert_w2[l])
        pending = (x1, y, gate)
    xt = _moe_combine(*pending)
    return xt.reshape(batch, seq, D_MODEL)
```

```python
import functools
import math

import jax
import jax.numpy as jnp
from jax import lax
from jax.experimental import pallas as pl
from jax.experimental.pallas import tpu as pltpu

D_MODEL = 1024
N_DIFF_HEADS = 4
DIFF_QK_DIM = 64
DIFF_V_DIM = 2 * DIFF_QK_DIM
ATTN_WIDTH = N_DIFF_HEADS * DIFF_V_DIM
SSD_HEAD_DIM = 64
SSD_WIDTH = D_MODEL // 2
SSD_HEADS = SSD_WIDTH // SSD_HEAD_DIM
SSD_GROUPS = 2
SSD_HEADS_PER_GROUP = SSD_HEADS // SSD_GROUPS
SSD_STATE = 64
SSD_CONV = 5
SSD_CHUNK = 128
SSD_BC = SSD_GROUPS * SSD_STATE
SSD_CONV_CH = SSD_WIDTH + 2 * SSD_BC
Q_COLS = N_DIFF_HEADS * 2 * DIFF_QK_DIM
DT_COLS = 2 * SSD_HEADS
FOURIER_GROUPS = 4
FOURIER_GROUP_DIM = D_MODEL // FOURIER_GROUPS
MOE_GROUPS = 4
MOE_EXPERTS_PER_GROUP = 8
MOE_EXPERTS = MOE_GROUPS * MOE_EXPERTS_PER_GROUP
MOE_TOP_K = 2
MOE_HIDDEN = 256
RMS_EPS = 1e-6
LOG2_E = math.log2(math.e)

V7X_LANES = 128
V7X_VMEM_LIMIT = 48 * 1024 * 1024
TOKEN_TILE = 512
ATTN_Q_TILE = 256
DFT_ROW_TILE = 512
MOE_ROWS = 256
ROW_SUBLANES = 8
DMA_UNROLL = 16
CONV_ROWS = 128

_F32 = jnp.float32
_BF16 = jnp.bfloat16
_NT = (((1,), (1,)), ((), ()))
_TN = (((0,), (0,)), ((), ()))


def _params(*sem):
    return pltpu.CompilerParams(dimension_semantics=sem, vmem_limit_bytes=V7X_VMEM_LIMIT)


def _tile(n, want):
    t = min(n, want)
    assert n % t == 0, (n, t)
    return t


def _full(shape):
    return pl.BlockSpec(shape, lambda *_: (0,) * len(shape))


def _rms_rows(x, w):
    ms = jnp.mean(x * x, axis=-1, keepdims=True)
    return x * lax.rsqrt(ms + RMS_EPS) * w


def _silu(x):
    return x * jax.nn.sigmoid(x)


def _softplus(x):
    return jnp.maximum(x, 0.0) + jnp.log1p(jnp.exp(-jnp.abs(x)))


def _in_proj_kernel(x_ref, nw_ref, wqkv_ref, wz_ref, wxbc_ref, wdt_ref, wdtt_ref, qn_ref, kn_ref,
                    q_ref, k_ref, v_ref, z_ref, xbc_ref, dt_ref, dtt_ref):
    tm = x_ref.shape[0]
    h = _rms_rows(x_ref[...], nw_ref[...]).astype(_BF16)
    qkv = jnp.dot(h, wqkv_ref[...], preferred_element_type=_F32)
    low = lax.broadcasted_iota(jnp.int32, (tm, V7X_LANES), 1) < DIFF_QK_DIM

    def qk_norm(slab, w):
        sq = slab * slab
        s_lo = jnp.sum(jnp.where(low, sq, 0.0), axis=-1, keepdims=True)
        s_hi = jnp.sum(jnp.where(low, 0.0, sq), axis=-1, keepdims=True)
        ms = jnp.where(low, s_lo, s_hi) * (1.0 / DIFF_QK_DIM)
        return slab * lax.rsqrt(ms + RMS_EPS) * w

    scale = DIFF_QK_DIM ** -0.5 * LOG2_E
    for hd in range(N_DIFF_HEADS):
        c0 = hd * DIFF_V_DIM
        q_ref[:, c0:c0 + DIFF_V_DIM] = (qk_norm(qkv[:, c0:c0 + DIFF_V_DIM], qn_ref[...]) * scale).astype(_BF16)
        k_ref[:, c0:c0 + DIFF_V_DIM] = qk_norm(qkv[:, Q_COLS + c0:Q_COLS + c0 + DIFF_V_DIM], kn_ref[...]).astype(_BF16)
    v_ref[...] = qkv[:, 2 * Q_COLS:].astype(_BF16)
    z_ref[...] = jnp.dot(h, wz_ref[...], preferred_element_type=_F32).astype(_BF16)
    xbc_ref[...] = jnp.dot(h, wxbc_ref[...], preferred_element_type=_F32)
    dt_ref[...] = jnp.dot(h, wdt_ref[...], preferred_element_type=_F32)
    dtt = lax.dot_general(wdtt_ref[...], h, _NT, preferred_element_type=_F32)
    for j in range(tm // SSD_CHUNK):
        dtt_ref[j] = dtt[:, j * SSD_CHUNK:(j + 1) * SSD_CHUNK]


def _in_proj(x2d, norm_w, w_in, q_norm, k_norm):
    t = x2d.shape[0]
    tm = _tile(t, TOKEN_TILE)
    c_qkv = 3 * Q_COLS
    c_z = c_qkv + SSD_WIDTH
    c_xbc = c_z + SSD_CONV_CH
    wb = w_in.astype(_BF16)
    wqkv, wz, wxbc, wdt = wb[:, :c_qkv], wb[:, c_qkv:c_z], wb[:, c_z:c_xbc], wb[:, c_xbc:]
    row = lambda i: (i, 0)
    outs = pl.pallas_call(
        _in_proj_kernel,
        grid=(t // tm,),
        in_specs=[pl.BlockSpec((tm, D_MODEL), row), _full((1, D_MODEL)),
                  _full((D_MODEL, c_qkv)), _full((D_MODEL, SSD_WIDTH)), _full((D_MODEL, SSD_CONV_CH)),
                  _full((D_MODEL, DT_COLS)), _full((DT_COLS, D_MODEL)),
                  _full((1, DIFF_V_DIM)), _full((1, DIFF_V_DIM))],
        out_specs=[pl.BlockSpec((tm, Q_COLS), row), pl.BlockSpec((tm, Q_COLS), row),
                   pl.BlockSpec((tm, ATTN_WIDTH), row), pl.BlockSpec((tm, SSD_WIDTH), row),
                   pl.BlockSpec((tm, SSD_CONV_CH), row), pl.BlockSpec((tm, DT_COLS), row),
                   pl.BlockSpec((tm // SSD_CHUNK, DT_COLS, SSD_CHUNK), lambda i: (i, 0, 0))],
        out_shape=[jax.ShapeDtypeStruct((t, Q_COLS), _BF16), jax.ShapeDtypeStruct((t, Q_COLS), _BF16),
                   jax.ShapeDtypeStruct((t, ATTN_WIDTH), _BF16), jax.ShapeDtypeStruct((t, SSD_WIDTH), _BF16),
                   jax.ShapeDtypeStruct((t, SSD_CONV_CH), _F32), jax.ShapeDtypeStruct((t, DT_COLS), _F32),
                   jax.ShapeDtypeStruct((t // SSD_CHUNK, DT_COLS, SSD_CHUNK), _F32)],
        compiler_params=_params("parallel"),
        name="in_proj",
    )(x2d, norm_w.reshape(1, D_MODEL), wqkv, wz, wxbc, wdt, wdt.T,
      jnp.tile(q_norm, 2).reshape(1, DIFF_V_DIM), jnp.tile(k_norm, 2).reshape(1, DIFF_V_DIM))
    return outs


def _attn_kernel(slope_ref, lq1_ref, lk1_ref, lq2_ref, lk2_ref, q_ref, k_ref, v_ref, subln_ref, o_ref,
                 bias_ref, *, lambda_init):
    tq = q_ref.shape[0]
    seq = k_ref.shape[0]
    hd = pl.program_id(0)
    qi = pl.program_id(1)

    @pl.when(pl.program_id(2) == 0)
    def _():
        row = qi * tq + lax.broadcasted_iota(jnp.int32, (tq, seq), 0)
        col = lax.broadcasted_iota(jnp.int32, (tq, seq), 1)
        bias_ref[...] = (slope_ref[hd] * LOG2_E) * jnp.abs(row - col).astype(_F32)

    lam = (jnp.exp(jnp.sum(lq1_ref[...] * lk1_ref[...], keepdims=True))
           - jnp.exp(jnp.sum(lq2_ref[...] * lk2_ref[...], keepdims=True)) + lambda_init)
    q = q_ref[...]
    k = k_ref[...]
    low = lax.broadcasted_iota(jnp.int32, (tq, V7X_LANES), 1) < DIFF_QK_DIM
    zero = jnp.zeros_like(q)

    def branch(qc):
        s = lax.dot_general(qc, k, _NT, preferred_element_type=_F32) - bias_ref[...]
        p = jnp.exp2(s - jnp.max(s, axis=-1, keepdims=True))
        return p, jnp.sum(p, axis=-1, keepdims=True)

    p1, l1 = branch(jnp.where(low, q, zero))
    p2, l2 = branch(jnp.where(low, zero, q))
    w = p1 * (1.0 / l1) - p2 * (lam / l2)
    o = jnp.dot(w.astype(_BF16), v_ref[...], preferred_element_type=_F32)
    o_ref[...] = (_rms_rows(o, subln_ref[...]) * (1.0 - lambda_init)).astype(_BF16)


def _diff_attention(q, k, v, lq1, lk1, lq2, lk2, subln, lambda_init, batch, seq):
    tq = _tile(seq, ATTN_Q_TILE)
    slopes = jnp.asarray([2.0 ** (-8.0 * (h + 1) / N_DIFF_HEADS) for h in range(N_DIFF_HEADS)], _F32)
    vec = lambda a: a.reshape(1, -1).astype(_F32)
    q3, k3, v3 = (a.reshape(batch, seq, ATTN_WIDTH) for a in (q, k, v))
    kv_spec = pl.BlockSpec((None, seq, DIFF_V_DIM), lambda h, i, b: (b, 0, h))
    q_spec = pl.BlockSpec((None, tq, DIFF_V_DIM), lambda h, i, b: (b, i, h))
    out = pl.pallas_call(
        functools.partial(_attn_kernel, lambda_init=lambda_init),
        grid=(N_DIFF_HEADS, seq // tq, batch),
        in_specs=[pl.BlockSpec(memory_space=pltpu.SMEM)] + [_full((1, DIFF_QK_DIM))] * 4
                 + [q_spec, kv_spec, kv_spec, _full((1, DIFF_V_DIM))],
        out_specs=q_spec,
        out_shape=jax.ShapeDtypeStruct((batch, seq, ATTN_WIDTH), _BF16),
        scratch_shapes=[pltpu.VMEM((tq, seq), _F32)],
        compiler_params=_params("parallel", "parallel", "arbitrary"),
        name="diff_attn",
    )(slopes, vec(lq1), vec(lk1), vec(lq2), vec(lk2), q3, k3, v3, vec(subln))
    return out.reshape(batch * seq, ATTN_WIDTH)


def _ssd_kernel(xbc_ref, z_ref, dt_ref, dtt_ref, cw_ref, cb_ref, dtb_row_ref, dtb_col_ref,
                alog_row_ref, alog_col_ref, dskip_ref, nw_ref, o_ref,
                xpad_ref, xc_ref, y_ref, state_ref):
    seq = xbc_ref.shape[0]
    n_chunks = seq // SSD_CHUNK
    L = SSD_CHUNK
    halo = 8
    pad = SSD_CONV // 2

    xpad_ref[0:halo, :] = jnp.zeros((halo, SSD_CONV_CH), _F32)
    xpad_ref[halo + seq:2 * halo + seq, :] = jnp.zeros((halo, SSD_CONV_CH), _F32)
    xpad_ref[halo:halo + seq, :] = xbc_ref[...]
    for rb in range(seq // CONV_ROWS):
        base = rb * CONV_ROWS + halo - pad
        acc = jnp.broadcast_to(cb_ref[...], (CONV_ROWS, SSD_CONV_CH))
        for j in range(SSD_CONV):
            acc = acc + cw_ref[j:j + 1, :] * xpad_ref[base + j:base + j + CONV_ROWS, :]
        xc_ref[rb * CONV_ROWS:(rb + 1) * CONV_ROWS, :] = _silu(acc).astype(_BF16)

    sub = lax.broadcasted_iota(jnp.int32, (L, L), 0)
    lane = lax.broadcasted_iota(jnp.int32, (L, L), 1)
    causal = lane <= sub
    anti = lane >= sub
    tri = causal.astype(_BF16)
    trit = anti.astype(_BF16)
    neg_big = jnp.float32(-1e30)
    head_of_col = lax.broadcasted_iota(jnp.int32, (DT_COLS, SSD_WIDTH), 1) // SSD_HEAD_DIM
    dtcol = lax.broadcasted_iota(jnp.int32, (DT_COLS, SSD_WIDTH), 0)
    expand_f = (dtcol == head_of_col).astype(_BF16)
    expand_b = (dtcol == head_of_col + SSD_HEADS).astype(_BF16)
    state_row_group = lax.broadcasted_iota(jnp.int32, (SSD_BC, SSD_WIDTH), 0) // SSD_STATE
    state_col_group = (lax.broadcasted_iota(jnp.int32, (SSD_BC, SSD_WIDTH), 1)
                       // (SSD_HEAD_DIM * SSD_HEADS_PER_GROUP))
    state_mask = state_row_group == state_col_group
    a_row = -jnp.exp(alog_row_ref[...])
    a_col = -jnp.exp(alog_col_ref[...])

    def split(a, n):
        parts = []
        for _ in range(n - 1):
            p = a.astype(_BF16)
            parts.append(p)
            a = a - p.astype(_F32)
        return parts + [a.astype(_BF16)]

    def cumsum_cols(a):
        return sum(jnp.dot(tri, p, preferred_element_type=_F32) for p in split(a, 3))

    def cumsum_rows(a):
        return sum(jnp.dot(p, trit, preferred_element_type=_F32) for p in split(a, 3))

    def spread(v, expand):
        return sum(jnp.dot(p, expand, preferred_element_type=_F32) for p in split(v, 2))

    def load_chunk(c):
        r0 = pl.multiple_of(c * L, L)
        xs = xc_ref[pl.ds(r0, L), 0:SSD_WIDTH]
        bc = xc_ref[pl.ds(r0, L), SSD_WIDTH:SSD_WIDTH + SSD_BC]
        cc = xc_ref[pl.ds(r0, L), SSD_WIDTH + SSD_BC:SSD_CONV_CH]
        dtc = _softplus(dt_ref[pl.ds(r0, L), :] + dtb_row_ref[...])
        a_c = dtc * a_row
        acs_c = cumsum_cols(a_c)
        return r0, xs, bc, cc, dtc, a_c, acs_c

    def state_step(state, bc, xs, weight_cols, total_row, expand):
        wexp = spread(weight_cols, expand)
        xw = (xs.astype(_F32) * wexp).astype(_BF16)
        new = lax.dot_general(bc, xw, _TN, preferred_element_type=_F32)
        decay = jnp.exp(spread(jnp.broadcast_to(total_row, (8, DT_COLS)), expand))[0:1, :]
        return state * decay + jnp.where(state_mask, new, 0.0)

    state_ref[...] = jnp.zeros_like(state_ref)

    def fwd_body(c, carry):
        r0, xs, bc, cc, dtc, a_c, acs_c = load_chunk(c)
        dtr = _softplus(dtt_ref[c] + dtb_col_ref[...])
        a_r = dtr * a_col
        acs_r = cumsum_rows(a_r)
        ecs_c = acs_c - a_c
        ecs_r = acs_r - a_r
        lane_bc = lax.broadcasted_iota(jnp.int32, (L, SSD_BC), 1)
        gmat = []
        for g in range(SSD_GROUPS):
            in_g = (lane_bc >= g * SSD_STATE) & (lane_bc < (g + 1) * SSD_STATE)
            cg = jnp.where(in_g, cc, jnp.zeros_like(cc))
            gmat.append(lax.dot_general(cg, bc, _NT, preferred_element_type=_F32))
        parts = []
        for h in range(SSD_HEADS):
            hb = SSD_HEADS + h
            lf = jnp.exp(jnp.where(causal, acs_c[:, h:h + 1] - acs_r[h:h + 1, :], neg_big))
            lb = jnp.exp(jnp.where(anti, ecs_r[hb:hb + 1, :] - ecs_c[:, hb:hb + 1], neg_big))
            w = gmat[h // SSD_HEADS_PER_GROUP] * (lf * dtr[h:h + 1, :] + lb * dtr[hb:hb + 1, :])
            parts.append(jnp.dot(w.astype(_BF16), xs[:, h * SSD_HEAD_DIM:(h + 1) * SSD_HEAD_DIM],
                                 preferred_element_type=_F32))
        y_diag = jnp.concatenate(parts, axis=-1)
        state = state_ref[...]
        y_off = (jnp.dot(cc, state.astype(_BF16), preferred_element_type=_F32)
                 * spread(jnp.exp(acs_c), expand_f))
        y_ref[pl.ds(r0, L), :] = y_diag + y_off
        total = acs_c[L - 1:L, :]
        state_ref[...] = state_step(state, bc, xs, jnp.exp(total - acs_c) * dtc, total, expand_f)
        return carry

    lax.fori_loop(0, n_chunks, fwd_body, 0, unroll=2)

    state_ref[...] = jnp.zeros_like(state_ref)

    def bwd_body(i, carry):
        c = n_chunks - 1 - i
        r0, xs, bc, cc, dtc, a_c, acs_c = load_chunk(c)
        ecs_c = acs_c - a_c
        total = acs_c[L - 1:L, :]
        state = state_ref[...]
        y_off = (jnp.dot(cc, state.astype(_BF16), preferred_element_type=_F32)
                 * spread(jnp.exp(total - ecs_c), expand_b))
        y = y_ref[pl.ds(r0, L), :] + y_off + dskip_ref[...] * xs.astype(_F32)
        gated = y * _silu(z_ref[pl.ds(r0, L), :].astype(_F32))
        o_ref[pl.ds(r0, L), :] = _rms_rows(gated, nw_ref[...]).astype(_BF16)
        state_ref[...] = state_step(state, bc, xs, jnp.exp(ecs_c) * dtc, total, expand_b)
        return carry

    lax.fori_loop(0, n_chunks, bwd_body, 0, unroll=2)


def _ssd_mixer(z, xbc, dt, dtt, conv_w, conv_b, dt_bias, a_log, d_skip, norm_w, batch, seq):
    n_chunks = seq // SSD_CHUNK
    per_b = lambda w: pl.BlockSpec((None, seq, w), lambda b: (b, 0, 0))
    dtb = dt_bias.reshape(DT_COLS).astype(_F32)
    alog = a_log.reshape(DT_COLS).astype(_F32)
    out = pl.pallas_call(
        _ssd_kernel,
        grid=(batch,),
        in_specs=[per_b(SSD_CONV_CH), per_b(SSD_WIDTH), per_b(DT_COLS),
                  pl.BlockSpec((n_chunks, DT_COLS, SSD_CHUNK), lambda b: (b, 0, 0)),
                  _full((SSD_CONV, SSD_CONV_CH)), _full((1, SSD_CONV_CH)),
                  _full((1, DT_COLS)), _full((DT_COLS, 1)), _full((1, DT_COLS)), _full((DT_COLS, 1)),
                  _full((1, SSD_WIDTH)), _full((1, SSD_WIDTH))],
        out_specs=per_b(SSD_WIDTH),
        out_shape=jax.ShapeDtypeStruct((batch, seq, SSD_WIDTH), _BF16),
        scratch_shapes=[pltpu.VMEM((seq + 16, SSD_CONV_CH), _F32), pltpu.VMEM((seq, SSD_CONV_CH), _BF16),
                        pltpu.VMEM((seq, SSD_WIDTH), _F32), pltpu.VMEM((SSD_BC, SSD_WIDTH), _F32)],
        compiler_params=_params("parallel"),
        name="ssd_mixer",
    )(xbc.reshape(batch, seq, SSD_CONV_CH), z.reshape(batch, seq, SSD_WIDTH), dt.reshape(batch, seq, DT_COLS),
      dtt, conv_w.astype(_F32), conv_b.reshape(1, SSD_CONV_CH).astype(_F32),
      dtb.reshape(1, DT_COLS), dtb.reshape(DT_COLS, 1), alog.reshape(1, DT_COLS), alog.reshape(DT_COLS, 1),
      jnp.repeat(d_skip.astype(_F32), SSD_HEAD_DIM).reshape(1, SSD_WIDTH), norm_w.reshape(1, SSD_WIDTH))
    return out.reshape(batch * seq, SSD_WIDTH)


def _proj_router_kernel(*refs, n_act):
    x_ref = refs[0]
    acts = refs[1:1 + n_act]
    ws = refs[1 + n_act:1 + 2 * n_act]
    nw_ref, wr_hi_ref, wr_lo_ref, br_ref, x1_ref, h_ref, eidx_ref, gate_ref = refs[1 + 2 * n_act:]
    tm = x_ref.shape[0]
    x1 = x_ref[...]
    for a_ref, w_ref in zip(acts, ws):
        x1 = x1 + jnp.dot(a_ref[...], w_ref[...], preferred_element_type=_F32)
    x1_ref[...] = x1
    h = _rms_rows(x1, nw_ref[...])
    _store_row_tiles(h_ref, h)

    h_hi = h.astype(_BF16)
    h_lo = (h - h_hi.astype(_F32)).astype(_BF16)
    logits = (jnp.dot(h_hi, wr_hi_ref[...], preferred_element_type=_F32)
              + jnp.dot(h_lo, wr_hi_ref[...], preferred_element_type=_F32)
              + jnp.dot(h_hi, wr_lo_ref[...], preferred_element_type=_F32) + br_ref[...])
    lane = lax.broadcasted_iota(jnp.int32, (tm, V7X_LANES), 1)
    lane_f = lane.astype(_F32)
    ninf = jnp.float32(-jnp.inf)
    nolane = jnp.float32(V7X_LANES)

    def argmax_first(vals):
        m = jnp.max(vals, axis=-1, keepdims=True)
        idx = jnp.min(jnp.where(vals == m, lane_f, nolane), axis=-1, keepdims=True)
        return m, idx

    gl = jnp.where(lane < MOE_GROUPS, logits, ninf)
    gmax, gidx = argmax_first(gl)
    g_p = 1.0 / jnp.sum(jnp.exp(gl - gmax), axis=-1, keepdims=True)
    first = MOE_GROUPS + MOE_EXPERTS_PER_GROUP * gidx
    el = jnp.where((lane_f >= first) & (lane_f < first + MOE_EXPERTS_PER_GROUP), logits, ninf)
    e1, i1 = argmax_first(el)
    e2, i2 = argmax_first(jnp.where(lane_f == i1, ninf, el))
    r = jnp.exp(e2 - e1)
    p1 = 1.0 / (1.0 + r)
    gate_ref[...] = jnp.where(lane == 0, g_p * p1, jnp.where(lane == 1, g_p * (r * p1), 0.0))
    eidx_ref[...] = jnp.where(lane == 0, i1 - MOE_GROUPS, jnp.where(lane == 1, i2 - MOE_GROUPS, 0.0)).astype(jnp.int32)


def _proj_router(x2d, acts, weights, norm_w, rg_w, rg_b, re_w, re_b):
    t = x2d.shape[0]
    tm = _tile(t, TOKEN_TILE)
    n_act = len(acts)
    n_log = MOE_GROUPS + MOE_EXPERTS
    wr = jnp.zeros((D_MODEL, V7X_LANES), _F32).at[:, :n_log].set(
        jnp.concatenate([rg_w.astype(_F32), re_w.reshape(D_MODEL, MOE_EXPERTS).astype(_F32)], axis=1))
    br = jnp.zeros((1, V7X_LANES), _F32).at[0, :n_log].set(
        jnp.concatenate([rg_b.astype(_F32), re_b.reshape(MOE_EXPERTS).astype(_F32)]))
    wr_hi = wr.astype(_BF16)
    wr_lo = (wr - wr_hi.astype(_F32)).astype(_BF16)
    row = lambda i: (i, 0)
    x1, h, eidx, gate = pl.pallas_call(
        functools.partial(_proj_router_kernel, n_act=n_act),
        grid=(t // tm,),
        in_specs=[pl.BlockSpec((tm, D_MODEL), row)]
                 + [pl.BlockSpec((tm, a.shape[1]), row) for a in acts]
                 + [_full(w.shape) for w in weights]
                 + [_full((1, D_MODEL)), _full((D_MODEL, V7X_LANES)), _full((D_MODEL, V7X_LANES)),
                    _full((1, V7X_LANES))],
        out_specs=[pl.BlockSpec((tm, D_MODEL), row), pl.BlockSpec((tm * ROW_SUBLANES, V7X_LANES), row),
                   pl.BlockSpec((tm, V7X_LANES), row), pl.BlockSpec((tm, V7X_LANES), row)],
        out_shape=[jax.ShapeDtypeStruct((t, D_MODEL), _F32),
                   jax.ShapeDtypeStruct((t * ROW_SUBLANES, V7X_LANES), _F32),
                   jax.ShapeDtypeStruct((t, V7X_LANES), jnp.int32), jax.ShapeDtypeStruct((t, V7X_LANES), _F32)],
        compiler_params=_params("parallel"),
        name="proj_router",
    )(x2d, *acts, *[w.astype(_BF16) for w in weights], norm_w.reshape(1, D_MODEL), wr_hi, wr_lo, br)
    return x1, h, eidx, gate


def _store_row_tiles(ref, val):
    n = val.shape[0]
    for s in range(ROW_SUBLANES):
        ref[pl.ds(s, n, stride=ROW_SUBLANES), :] = val[:, s * V7X_LANES:(s + 1) * V7X_LANES]


def _load_row_tiles(ref, n):
    return jnp.concatenate([ref[pl.ds(s, n, stride=ROW_SUBLANES), :] for s in range(ROW_SUBLANES)], axis=1)


def _expert_kernel(bexp_ref, nused_ref, gidx_ref, gnext_ref, sidx_ref, h_hbm, w1_ref, w3_ref, w2_ref, out_hbm,
                   xbuf, ybuf, w13_bf, w2_bf, sem_in, sem_out):
    i = pl.program_id(0)

    @pl.when((i == 0) | (bexp_ref[i] != bexp_ref[jnp.maximum(i - 1, 0)]))
    def _():
        w13_bf[:, :MOE_HIDDEN] = w1_ref[...].astype(_BF16)
        w13_bf[:, MOE_HIDDEN:] = w3_ref[...].astype(_BF16)
        w2_bf[...] = w2_ref[...].astype(_BF16)

    n_used = nused_ref[0]
    slot = lax.rem(i, 2)
    other = 1 - slot
    tile = ROW_SUBLANES

    def gather_copy(buf_slot, r, tok):
        return pltpu.make_async_copy(h_hbm.at[pl.ds(pl.multiple_of(tok * tile, tile), tile)],
                                     xbuf.at[buf_slot, pl.ds(pl.multiple_of(r * tile, tile), tile)],
                                     sem_in.at[buf_slot])

    def scatter_copy(buf_slot, r, row):
        return pltpu.make_async_copy(ybuf.at[buf_slot, pl.ds(pl.multiple_of(r * tile, tile), tile)],
                                     out_hbm.at[pl.ds(pl.multiple_of(row * tile, tile), tile)],
                                     sem_out.at[buf_slot])

    def for_rows(fn):
        def body(j, carry):
            for u in range(DMA_UNROLL):
                fn(j * DMA_UNROLL + u)
            return carry
        lax.fori_loop(0, MOE_ROWS // DMA_UNROLL, body, 0)

    def start_gather(idx_ref, buf_slot):
        for_rows(lambda r: gather_copy(buf_slot, r, idx_ref[0, r]).start())

    def wait_gather(buf_slot):
        for r in range(MOE_ROWS):
            gather_copy(buf_slot, r, 0).wait()

    def start_scatter(buf_slot):
        for_rows(lambda r: scatter_copy(buf_slot, r, sidx_ref[0, r]).start())

    def wait_scatter(buf_slot):
        for r in range(MOE_ROWS):
            scatter_copy(buf_slot, r, 0).wait()

    @pl.when(i == 0)
    def _():
        spare0 = out_hbm.shape[0] - 2 * MOE_ROWS * tile
        ybuf[...] = jnp.zeros_like(ybuf)
        fills = [pltpu.make_async_copy(ybuf.at[s], out_hbm.at[pl.ds(spare0 + s * MOE_ROWS * tile, MOE_ROWS * tile)],
                                       sem_out.at[s]) for s in range(2)]
        for f in fills:
            f.start()
        for f in fills:
            f.wait()
        start_gather(gidx_ref, 0)

    @pl.when(i + 1 < n_used)
    def _():
        start_gather(gnext_ref, other)

    @pl.when(i < n_used)
    def _():
        wait_gather(slot)
        xb = _load_row_tiles(xbuf.at[slot], MOE_ROWS).astype(_BF16)
        a = jnp.dot(xb, w13_bf[...], preferred_element_type=_F32)
        hid = _silu(a[:, :MOE_HIDDEN]) * a[:, MOE_HIDDEN:]
        y = jnp.dot(hid.astype(_BF16), w2_bf[...], preferred_element_type=_F32)

        @pl.when(i >= 2)
        def _():
            wait_scatter(slot)

        _store_row_tiles(ybuf.at[slot], y)
        start_scatter(slot)

        @pl.when(i == n_used - 1)
        def _():
            wait_scatter(slot)

            @pl.when(i >= 1)
            def _():
                wait_scatter(other)


def _expert_plan(eidx, n_tok):
    n_assign = n_tok * MOE_TOP_K
    n_blocks = n_assign // MOE_ROWS + MOE_EXPERTS
    flat_e = eidx.reshape(n_assign)
    order = jnp.argsort(flat_e).astype(jnp.int32)
    tok_sorted = order // MOE_TOP_K
    row_sorted = (order % MOE_TOP_K) * n_tok + tok_sorted
    counts = jnp.bincount(flat_e, length=MOE_EXPERTS).astype(jnp.int32)
    seg_start = jnp.cumsum(counts) - counts
    nblk = (counts + MOE_ROWS - 1) // MOE_ROWS
    blk_end = jnp.cumsum(nblk)
    blk_first = blk_end - nblk
    n_used = blk_end[-1]
    bid = jnp.arange(n_blocks, dtype=jnp.int32)
    used = bid < n_used
    bexp = jnp.minimum(jnp.sum(bid[:, None] >= blk_end[None, :], axis=1), MOE_EXPERTS - 1).astype(jnp.int32)
    last_exp = jnp.max(jnp.where(counts > 0, jnp.arange(MOE_EXPERTS, dtype=jnp.int32), 0))
    bexp = jnp.where(used, bexp, last_exp)
    local = bid - blk_first[bexp]
    bstart = seg_start[bexp] + local * MOE_ROWS
    bvalid = jnp.where(used, jnp.clip(counts[bexp] - local * MOE_ROWS, 0, MOE_ROWS), 0)
    r = jnp.arange(MOE_ROWS, dtype=jnp.int32)
    valid = r[None, :] < bvalid[:, None]
    src = jnp.clip(bstart[:, None] + r[None, :], 0, n_assign - 1)
    gidx = jnp.where(valid, tok_sorted[src], 0).astype(jnp.int32)
    spare = MOE_TOP_K * n_tok + (bid[:, None] % 2) * MOE_ROWS + r[None, :]
    sidx = jnp.where(valid, row_sorted[src], spare).astype(jnp.int32)
    shape3 = (n_blocks, 1, MOE_ROWS)
    return bexp, n_used.reshape(1).astype(jnp.int32), gidx.reshape(shape3), sidx.reshape(shape3), n_blocks


def _experts(h_tiles, eidx, w1, w3, w2):
    n_tok = h_tiles.shape[0] // ROW_SUBLANES
    bexp, n_used, gidx, sidx, n_blocks = _expert_plan(eidx, n_tok)
    idx_block = (None, 1, MOE_ROWS)
    per_expert = lambda i, bexp, nu: (bexp[i], 0, 0)
    grid_spec = pltpu.PrefetchScalarGridSpec(
        num_scalar_prefetch=2,
        grid=(n_blocks,),
        in_specs=[pl.BlockSpec(idx_block, lambda i, bexp, nu: (i, 0, 0), memory_space=pltpu.SMEM),
                  pl.BlockSpec(idx_block, lambda i, bexp, nu: (jnp.minimum(i + 1, n_blocks - 1), 0, 0),
                               memory_space=pltpu.SMEM),
                  pl.BlockSpec(idx_block, lambda i, bexp, nu: (i, 0, 0), memory_space=pltpu.SMEM),
                  pl.BlockSpec(memory_space=pl.ANY),
                  pl.BlockSpec((None, D_MODEL, MOE_HIDDEN), per_expert),
                  pl.BlockSpec((None, D_MODEL, MOE_HIDDEN), per_expert),
                  pl.BlockSpec((None, MOE_HIDDEN, D_MODEL), per_expert)],
        out_specs=pl.BlockSpec(memory_space=pl.ANY),
        scratch_shapes=[pltpu.VMEM((2, MOE_ROWS * ROW_SUBLANES, V7X_LANES), _F32),
                        pltpu.VMEM((2, MOE_ROWS * ROW_SUBLANES, V7X_LANES), _F32),
                        pltpu.VMEM((D_MODEL, 2 * MOE_HIDDEN), _BF16), pltpu.VMEM((MOE_HIDDEN, D_MODEL), _BF16),
                        pltpu.SemaphoreType.DMA((2,)), pltpu.SemaphoreType.DMA((2,))],
    )
    out_rows = MOE_TOP_K * n_tok + 2 * MOE_ROWS
    return pl.pallas_call(
        _expert_kernel,
        grid_spec=grid_spec,
        out_shape=jax.ShapeDtypeStruct((out_rows * ROW_SUBLANES, V7X_LANES), _F32),
        compiler_params=_params("arbitrary"),
        name="experts",
    )(bexp, n_used, gidx, gidx, sidx, h_tiles, w1, w3, w2)


def _combine(x1_ref, y0_ref, y1_ref, gate_ref):
    gate = gate_ref[...]
    tm = x1_ref.shape[0]
    return x1_ref[...] + gate[:, 0:1] * _load_row_tiles(y0_ref, tm) + gate[:, 1:2] * _load_row_tiles(y1_ref, tm)


def _combine_kernel(x1_ref, y0_ref, y1_ref, gate_ref, o_ref):
    o_ref[...] = _combine(x1_ref, y0_ref, y1_ref, gate_ref)


def _combine_dft_kernel(x1_ref, y0_ref, y1_ref, gate_ref, nw_ref, cosc_ref, sinc_ref, x_ref, hc_ref, hs_ref):
    x = _combine(x1_ref, y0_ref, y1_ref, gate_ref)
    x_ref[...] = x
    h = _rms_rows(x, nw_ref[...]).astype(_BF16)
    for g in range(FOURIER_GROUPS):
        c0 = g * FOURIER_GROUP_DIM
        hg = h[:, c0:c0 + FOURIER_GROUP_DIM]
        hc_ref[:, c0:c0 + FOURIER_GROUP_DIM] = jnp.dot(hg, cosc_ref[...], preferred_element_type=_F32).astype(_BF16)
        hs_ref[:, c0:c0 + FOURIER_GROUP_DIM] = jnp.dot(hg, sinc_ref[...], preferred_element_type=_F32).astype(_BF16)


def _moe_combine(x1, y, gate, dft=None):
    t = x1.shape[0]
    tm = _tile(t, TOKEN_TILE)
    row = lambda i: (i, 0)
    second = lambda i: (i + t // tm, 0)
    y_block = (tm * ROW_SUBLANES, V7X_LANES)
    in_specs = [pl.BlockSpec((tm, D_MODEL), row), pl.BlockSpec(y_block, row),
                pl.BlockSpec(y_block, second), pl.BlockSpec((tm, V7X_LANES), row)]
    if dft is None:
        return pl.pallas_call(
            _combine_kernel, grid=(t // tm,), in_specs=in_specs,
            out_specs=pl.BlockSpec((tm, D_MODEL), row),
            out_shape=jax.ShapeDtypeStruct((t, D_MODEL), _F32),
            compiler_params=_params("parallel"), name="moe_combine",
        )(x1, y, y, gate)
    norm_w, cosc, sinc = dft
    return pl.pallas_call(
        _combine_dft_kernel, grid=(t // tm,),
        in_specs=in_specs + [_full((1, D_MODEL)), _full(cosc.shape), _full(sinc.shape)],
        out_specs=[pl.BlockSpec((tm, D_MODEL), row)] * 3,
        out_shape=[jax.ShapeDtypeStruct((t, D_MODEL), _F32), jax.ShapeDtypeStruct((t, D_MODEL), _BF16),
                   jax.ShapeDtypeStruct((t, D_MODEL), _BF16)],
        compiler_params=_params("parallel"), name="moe_combine_dft",
    )(x1, y, y, gate, norm_w.reshape(1, D_MODEL), cosc, sinc)


def _seq_dft_kernel(cos_ref, nsin_ref, hc_ref, hs_ref, o_ref):
    o_ref[...] = (jnp.dot(cos_ref[...], hc_ref[...], preferred_element_type=_F32)
                  + jnp.dot(nsin_ref[...], hs_ref[...], preferred_element_type=_F32)).astype(_BF16)


def _seq_dft(hc, hs, cos_s, nsin_s, batch, seq):
    tq = _tile(seq, DFT_ROW_TILE)
    tab = pl.BlockSpec((tq, seq), lambda b, i: (i, 0))
    per_b = pl.BlockSpec((None, seq, D_MODEL), lambda b, i: (b, 0, 0))
    out = pl.pallas_call(
        _seq_dft_kernel, grid=(batch, seq // tq),
        in_specs=[tab, tab, per_b, per_b],
        out_specs=pl.BlockSpec((None, tq, D_MODEL), lambda b, i: (b, i, 0)),
        out_shape=jax.ShapeDtypeStruct((batch, seq, D_MODEL), _BF16),
        compiler_params=_params("parallel", "parallel"), name="seq_dft",
    )(cos_s, nsin_s, hc.reshape(batch, seq, D_MODEL), hs.reshape(batch, seq, D_MODEL))
    return out.reshape(batch * seq, D_MODEL)


def _dft_tables(n, scale):
    idx = jnp.arange(n, dtype=jnp.int32)
    m = (idx[:, None] * idx[None, :]) % n
    ang = m.astype(_F32) * (2.0 * math.pi / n)
    return jnp.cos(ang) * scale, jnp.sin(ang) * scale


def kernel(x, norm_mix, norm_ffn, w_in, q_norm, k_norm, lambda_q1, lambda_k1, lambda_q2, lambda_k2,
           attn_subln, conv_w, conv_b, dt_bias, a_log, d_skip, ssd_norm, w_out, w_fourier,
           router_group_w, router_group_b, router_expert_w, router_expert_b,
           expert_w1, expert_w3, expert_w2):
    batch, seq, _ = x.shape
    depth = norm_mix.shape[0]
    xt = x.reshape(batch * seq, D_MODEL)
    pending = None
    for l in range(depth):
        i = l // 2
        if l % 2 == 0:
            if pending is not None:
                xt = _moe_combine(*pending)
            lambda_init = 0.8 - 0.6 * math.exp(-0.3 * l)
            q, k, v, z, xbc, dt, dtt = _in_proj(xt, norm_mix[l], w_in[i], q_norm[i], k_norm[i])
            attn = _diff_attention(q, k, v, lambda_q1[i], lambda_k1[i], lambda_q2[i], lambda_k2[i],
                                   attn_subln[i], lambda_init, batch, seq)
            ssd = _ssd_mixer(z, xbc, dt, dtt, conv_w[i], conv_b[i], dt_bias[i], a_log[i], d_skip[i],
                             ssd_norm[i], batch, seq)
            acts = [attn, ssd]
            weights = [w_out[i][:ATTN_WIDTH], w_out[i][ATTN_WIDTH:]]
        else:
            cos_c, sin_c = _dft_tables(FOURIER_GROUP_DIM, FOURIER_GROUP_DIM ** -0.5)
            cos_s, sin_s = _dft_tables(seq, seq ** -0.5)
            dft = (norm_mix[l], cos_c.astype(_BF16), sin_c.astype(_BF16))
            if pending is None:
                zeros = jnp.zeros((2 * batch * seq * ROW_SUBLANES, V7X_LANES), _F32)
                pending = (xt, zeros, jnp.zeros((batch * seq, V7X_LANES), _F32))
            xt, hc, hs = _moe_combine(*pending, dft=dft)
            f = _seq_dft(hc, hs, cos_s.astype(_BF16), (-sin_s).astype(_BF16), batch, seq)
            acts = [f]
            weights = [w_fourier[i]]
        x1, h, eidx, gate = _proj_router(xt, acts, weights, norm_ffn[l], router_group_w[l], router_group_b[l],
                                         router_expert_w[l], router_expert_b[l])
        y = _experts(h, eidx[:, :MOE_TOP_K], expert_w1[l], expert_w3[l], expert_w2[l])
        pending = (x1, y, gate)
    xt = _moe_combine(*pending)
    return xt.reshape(batch, seq, D_MODEL)
```

```python
import functools
import math

import jax
import jax.numpy as jnp
from jax import lax
from jax.experimental import pallas as pl
from jax.experimental.pallas import tpu as pltpu

D_MODEL = 1024
N_DIFF_HEADS = 4
DIFF_QK_DIM = 64
DIFF_V_DIM = 2 * DIFF_QK_DIM
ATTN_WIDTH = N_DIFF_HEADS * DIFF_V_DIM
SSD_HEAD_DIM = 64
SSD_WIDTH = D_MODEL // 2
SSD_HEADS = SSD_WIDTH // SSD_HEAD_DIM
SSD_GROUPS = 2
SSD_HEADS_PER_GROUP = SSD_HEADS // SSD_GROUPS
SSD_STATE = 64
SSD_CONV = 5
SSD_CHUNK = 128
SSD_BC = SSD_GROUPS * SSD_STATE
SSD_CONV_CH = SSD_WIDTH + 2 * SSD_BC
Q_COLS = N_DIFF_HEADS * 2 * DIFF_QK_DIM
DT_COLS = 2 * SSD_HEADS
FOURIER_GROUPS = 4
FOURIER_GROUP_DIM = D_MODEL // FOURIER_GROUPS
MOE_GROUPS = 4
MOE_EXPERTS_PER_GROUP = 8
MOE_EXPERTS = MOE_GROUPS * MOE_EXPERTS_PER_GROUP
MOE_TOP_K = 2
MOE_HIDDEN = 256
RMS_EPS = 1e-6
LOG2_E = math.log2(math.e)

V7X_LANES = 128
V7X_VMEM_LIMIT = 48 * 1024 * 1024
TOKEN_TILE = 512
ATTN_Q_TILE = 256
DFT_ROW_TILE = 512
MOE_ROWS = 256
V7X_SUBLANES = 8
ROW_SUBLANES = D_MODEL // V7X_LANES
DFT_FINE = 64
DMA_UNROLL = 16
CONV_ROWS = 128

_F32 = jnp.float32
_BF16 = jnp.bfloat16
_NT = (((1,), (1,)), ((), ()))
_TN = (((0,), (0,)), ((), ()))


def _params(*sem):
    return pltpu.CompilerParams(dimension_semantics=sem, vmem_limit_bytes=V7X_VMEM_LIMIT)


def _tile(n, want):
    t = min(n, want)
    assert n % t == 0, (n, t)
    return t


def _full(shape):
    return pl.BlockSpec(shape, lambda *_: (0,) * len(shape))


def _rms_rows(x, w):
    ms = jnp.mean(x * x, axis=-1, keepdims=True)
    return x * lax.rsqrt(ms + RMS_EPS) * w


def _silu(x):
    return x * jax.nn.sigmoid(x)


def _softplus(x):
    return jnp.maximum(x, 0.0) + jnp.log1p(jnp.exp(-jnp.abs(x)))


def _in_proj_kernel(x_ref, nw_ref, wqkv_ref, wz_ref, wxbc_ref, wdt_ref, wdtt_ref, qn_ref, kn_ref,
                    q_ref, k_ref, v_ref, z_ref, xbc_ref, dt_ref, dtt_ref):
    tm = x_ref.shape[0]
    h = _rms_rows(x_ref[...], nw_ref[...]).astype(_BF16)
    qkv = jnp.dot(h, wqkv_ref[...], preferred_element_type=_F32)
    low = lax.broadcasted_iota(jnp.int32, (tm, V7X_LANES), 1) < DIFF_QK_DIM

    def qk_norm(slab, w):
        sq = slab * slab
        s_lo = jnp.sum(jnp.where(low, sq, 0.0), axis=-1, keepdims=True)
        s_hi = jnp.sum(jnp.where(low, 0.0, sq), axis=-1, keepdims=True)
        ms = jnp.where(low, s_lo, s_hi) * (1.0 / DIFF_QK_DIM)
        return slab * lax.rsqrt(ms + RMS_EPS) * w

    scale = DIFF_QK_DIM ** -0.5 * LOG2_E
    for hd in range(N_DIFF_HEADS):
        c0 = hd * DIFF_V_DIM
        q_ref[:, c0:c0 + DIFF_V_DIM] = (qk_norm(qkv[:, c0:c0 + DIFF_V_DIM], qn_ref[...]) * scale).astype(_BF16)
        k_ref[:, c0:c0 + DIFF_V_DIM] = qk_norm(qkv[:, Q_COLS + c0:Q_COLS + c0 + DIFF_V_DIM], kn_ref[...]).astype(_BF16)
    v_ref[...] = qkv[:, 2 * Q_COLS:].astype(_BF16)
    z_ref[...] = jnp.dot(h, wz_ref[...], preferred_element_type=_F32).astype(_BF16)
    xbc_ref[...] = jnp.dot(h, wxbc_ref[...], preferred_element_type=_F32)
    dt_ref[...] = jnp.dot(h, wdt_ref[...], preferred_element_type=_F32)
    dtt = lax.dot_general(wdtt_ref[...], h, _NT, preferred_element_type=_F32)
    for j in range(tm // SSD_CHUNK):
        dtt_ref[j] = dtt[:, j * SSD_CHUNK:(j + 1) * SSD_CHUNK]


def _in_proj(x2d, norm_w, w_in, q_norm, k_norm):
    t = x2d.shape[0]
    tm = _tile(t, TOKEN_TILE)
    c_qkv = 3 * Q_COLS
    c_z = c_qkv + SSD_WIDTH
    c_xbc = c_z + SSD_CONV_CH
    wb = w_in.astype(_BF16)
    wqkv, wz, wxbc, wdt = wb[:, :c_qkv], wb[:, c_qkv:c_z], wb[:, c_z:c_xbc], wb[:, c_xbc:]
    row = lambda i: (i, 0)
    outs = pl.pallas_call(
        _in_proj_kernel,
        grid=(t // tm,),
        in_specs=[pl.BlockSpec((tm, D_MODEL), row), _full((1, D_MODEL)),
                  _full((D_MODEL, c_qkv)), _full((D_MODEL, SSD_WIDTH)), _full((D_MODEL, SSD_CONV_CH)),
                  _full((D_MODEL, DT_COLS)), _full((DT_COLS, D_MODEL)),
                  _full((1, DIFF_V_DIM)), _full((1, DIFF_V_DIM))],
        out_specs=[pl.BlockSpec((tm, Q_COLS), row), pl.BlockSpec((tm, Q_COLS), row),
                   pl.BlockSpec((tm, ATTN_WIDTH), row), pl.BlockSpec((tm, SSD_WIDTH), row),
                   pl.BlockSpec((tm, SSD_CONV_CH), row), pl.BlockSpec((tm, DT_COLS), row),
                   pl.BlockSpec((tm // SSD_CHUNK, DT_COLS, SSD_CHUNK), lambda i: (i, 0, 0))],
        out_shape=[jax.ShapeDtypeStruct((t, Q_COLS), _BF16), jax.ShapeDtypeStruct((t, Q_COLS), _BF16),
                   jax.ShapeDtypeStruct((t, ATTN_WIDTH), _BF16), jax.ShapeDtypeStruct((t, SSD_WIDTH), _BF16),
                   jax.ShapeDtypeStruct((t, SSD_CONV_CH), _F32), jax.ShapeDtypeStruct((t, DT_COLS), _F32),
                   jax.ShapeDtypeStruct((t // SSD_CHUNK, DT_COLS, SSD_CHUNK), _F32)],
        compiler_params=_params("parallel"),
        name="in_proj",
    )(x2d, norm_w.reshape(1, D_MODEL), wqkv, wz, wxbc, wdt, wdt.T,
      jnp.tile(q_norm, 2).reshape(1, DIFF_V_DIM), jnp.tile(k_norm, 2).reshape(1, DIFF_V_DIM))
    return outs


def _attn_kernel(slope_ref, lq1_ref, lk1_ref, lq2_ref, lk2_ref, q_ref, k_ref, v_ref, subln_ref, o_ref,
                 bias_ref, *, lambda_init):
    tq = q_ref.shape[0]
    seq = k_ref.shape[0]
    hd = pl.program_id(0)
    qi = pl.program_id(1)

    @pl.when(pl.program_id(2) == 0)
    def _():
        row = qi * tq + lax.broadcasted_iota(jnp.int32, (tq, seq), 0)
        col = lax.broadcasted_iota(jnp.int32, (tq, seq), 1)
        bias_ref[...] = (slope_ref[hd] * LOG2_E) * jnp.abs(row - col).astype(_F32)

    lam = (jnp.exp(jnp.sum(lq1_ref[...] * lk1_ref[...], keepdims=True))
           - jnp.exp(jnp.sum(lq2_ref[...] * lk2_ref[...], keepdims=True)) + lambda_init)
    q = q_ref[...]
    k = k_ref[...]
    low = lax.broadcasted_iota(jnp.int32, (tq, V7X_LANES), 1) < DIFF_QK_DIM
    zero = jnp.zeros_like(q)

    def branch(qc):
        s = lax.dot_general(qc, k, _NT, preferred_element_type=_F32) - bias_ref[...]
        p = jnp.exp2(s - jnp.max(s, axis=-1, keepdims=True))
        return p, jnp.sum(p, axis=-1, keepdims=True)

    p1, l1 = branch(jnp.where(low, q, zero))
    p2, l2 = branch(jnp.where(low, zero, q))
    w = p1 * (1.0 / l1) - p2 * (lam / l2)
    o = jnp.dot(w.astype(_BF16), v_ref[...], preferred_element_type=_F32)
    o_ref[...] = (_rms_rows(o, subln_ref[...]) * (1.0 - lambda_init)).astype(_BF16)


def _diff_attention(q, k, v, lq1, lk1, lq2, lk2, subln, lambda_init, batch, seq):
    tq = _tile(seq, ATTN_Q_TILE)
    slopes = jnp.asarray([2.0 ** (-8.0 * (h + 1) / N_DIFF_HEADS) for h in range(N_DIFF_HEADS)], _F32)
    vec = lambda a: a.reshape(1, -1).astype(_F32)
    q3, k3, v3 = (a.reshape(batch, seq, ATTN_WIDTH) for a in (q, k, v))
    kv_spec = pl.BlockSpec((None, seq, DIFF_V_DIM), lambda h, i, b: (b, 0, h))
    q_spec = pl.BlockSpec((None, tq, DIFF_V_DIM), lambda h, i, b: (b, i, h))
    out = pl.pallas_call(
        functools.partial(_attn_kernel, lambda_init=lambda_init),
        grid=(N_DIFF_HEADS, seq // tq, batch),
        in_specs=[pl.BlockSpec(memory_space=pltpu.SMEM)] + [_full((1, DIFF_QK_DIM))] * 4
                 + [q_spec, kv_spec, kv_spec, _full((1, DIFF_V_DIM))],
        out_specs=q_spec,
        out_shape=jax.ShapeDtypeStruct((batch, seq, ATTN_WIDTH), _BF16),
        scratch_shapes=[pltpu.VMEM((tq, seq), _F32)],
        compiler_params=_params("parallel", "parallel", "arbitrary"),
        name="diff_attn",
    )(slopes, vec(lq1), vec(lk1), vec(lq2), vec(lk2), q3, k3, v3, vec(subln))
    return out.reshape(batch * seq, ATTN_WIDTH)


def _ssd_kernel(xbc_ref, z_ref, dt_ref, dtt_ref, cw_ref, cb_ref, dtb_row_ref, dtb_col_ref,
                alog_row_ref, alog_col_ref, dskip_ref, nw_ref, o_ref,
                xpad_ref, xc_ref, y_ref, state_ref):
    seq = xbc_ref.shape[0]
    n_chunks = seq // SSD_CHUNK
    L = SSD_CHUNK
    halo = 8
    pad = SSD_CONV // 2

    xpad_ref[0:halo, :] = jnp.zeros((halo, SSD_CONV_CH), _F32)
    xpad_ref[halo + seq:2 * halo + seq, :] = jnp.zeros((halo, SSD_CONV_CH), _F32)
    xpad_ref[halo:halo + seq, :] = xbc_ref[...]
    for rb in range(seq // CONV_ROWS):
        base = rb * CONV_ROWS + halo - pad
        acc = jnp.broadcast_to(cb_ref[...], (CONV_ROWS, SSD_CONV_CH))
        for j in range(SSD_CONV):
            acc = acc + cw_ref[j:j + 1, :] * xpad_ref[base + j:base + j + CONV_ROWS, :]
        xc_ref[rb * CONV_ROWS:(rb + 1) * CONV_ROWS, :] = _silu(acc).astype(_BF16)

    sub = lax.broadcasted_iota(jnp.int32, (L, L), 0)
    lane = lax.broadcasted_iota(jnp.int32, (L, L), 1)
    causal = lane <= sub
    anti = lane >= sub
    tri = causal.astype(_BF16)
    trit = anti.astype(_BF16)
    neg_big = jnp.float32(-1e30)
    head_of_col = lax.broadcasted_iota(jnp.int32, (DT_COLS, SSD_WIDTH), 1) // SSD_HEAD_DIM
    dtcol = lax.broadcasted_iota(jnp.int32, (DT_COLS, SSD_WIDTH), 0)
    expand_f = (dtcol == head_of_col).astype(_BF16)
    expand_b = (dtcol == head_of_col + SSD_HEADS).astype(_BF16)
    state_row_group = lax.broadcasted_iota(jnp.int32, (SSD_BC, SSD_WIDTH), 0) // SSD_STATE
    state_col_group = (lax.broadcasted_iota(jnp.int32, (SSD_BC, SSD_WIDTH), 1)
                       // (SSD_HEAD_DIM * SSD_HEADS_PER_GROUP))
    state_mask = state_row_group == state_col_group
    a_row = -jnp.exp(alog_row_ref[...])
    a_col = -jnp.exp(alog_col_ref[...])

    def split(a, n):
        parts = []
        for _ in range(n - 1):
            p = a.astype(_BF16)
            parts.append(p)
            a = a - p.astype(_F32)
        return parts + [a.astype(_BF16)]

    def cumsum_cols(a):
        return sum(jnp.dot(tri, p, preferred_element_type=_F32) for p in split(a, 3))

    def cumsum_rows(a):
        return sum(jnp.dot(p, trit, preferred_element_type=_F32) for p in split(a, 3))

    def spread(v, expand):
        return sum(jnp.dot(p, expand, preferred_element_type=_F32) for p in split(v, 2))

    def load_chunk(c):
        r0 = pl.multiple_of(c * L, L)
        xs = xc_ref[pl.ds(r0, L), 0:SSD_WIDTH]
        bc = xc_ref[pl.ds(r0, L), SSD_WIDTH:SSD_WIDTH + SSD_BC]
        cc = xc_ref[pl.ds(r0, L), SSD_WIDTH + SSD_BC:SSD_CONV_CH]
        dtc = _softplus(dt_ref[pl.ds(r0, L), :] + dtb_row_ref[...])
        a_c = dtc * a_row
        acs_c = cumsum_cols(a_c)
        return r0, xs, bc, cc, dtc, a_c, acs_c

    def state_step(state, bc, xs, weight_cols, total_row, expand):
        wexp = spread(weight_cols, expand)
        xw = (xs.astype(_F32) * wexp).astype(_BF16)
        new = lax.dot_general(bc, xw, _TN, preferred_element_type=_F32)
        decay = jnp.exp(spread(jnp.broadcast_to(total_row, (8, DT_COLS)), expand))[0:1, :]
        return state * decay + jnp.where(state_mask, new, 0.0)

    state_ref[...] = jnp.zeros_like(state_ref)

    def fwd_body(c, carry):
        r0, xs, bc, cc, dtc, a_c, acs_c = load_chunk(c)
        dtr = _softplus(dtt_ref[c] + dtb_col_ref[...])
        a_r = dtr * a_col
        acs_r = cumsum_rows(a_r)
        ecs_c = acs_c - a_c
        ecs_r = acs_r - a_r
        lane_bc = lax.broadcasted_iota(jnp.int32, (L, SSD_BC), 1)
        gmat = []
        for g in range(SSD_GROUPS):
            in_g = (lane_bc >= g * SSD_STATE) & (lane_bc < (g + 1) * SSD_STATE)
            cg = jnp.where(in_g, cc, jnp.zeros_like(cc))
            gmat.append(lax.dot_general(cg, bc, _NT, preferred_element_type=_F32))
        parts = []
        for h in range(SSD_HEADS):
            hb = SSD_HEADS + h
            lf = jnp.exp(jnp.where(causal, acs_c[:, h:h + 1] - acs_r[h:h + 1, :], neg_big))
            lb = jnp.exp(jnp.where(anti, ecs_r[hb:hb + 1, :] - ecs_c[:, hb:hb + 1], neg_big))
            w = gmat[h // SSD_HEADS_PER_GROUP] * (lf * dtr[h:h + 1, :] + lb * dtr[hb:hb + 1, :])
            parts.append(jnp.dot(w.astype(_BF16), xs[:, h * SSD_HEAD_DIM:(h + 1) * SSD_HEAD_DIM],
                                 preferred_element_type=_F32))
        y_diag = jnp.concatenate(parts, axis=-1)
        state = state_ref[...]
        y_off = (jnp.dot(cc, state.astype(_BF16), preferred_element_type=_F32)
                 * spread(jnp.exp(acs_c), expand_f))
        y_ref[pl.ds(r0, L), :] = y_diag + y_off
        total = acs_c[L - 1:L, :]
        state_ref[...] = state_step(state, bc, xs, jnp.exp(total - acs_c) * dtc, total, expand_f)
        return carry

    lax.fori_loop(0, n_chunks, fwd_body, 0, unroll=2)

    state_ref[...] = jnp.zeros_like(state_ref)

    def bwd_body(i, carry):
        c = n_chunks - 1 - i
        r0, xs, bc, cc, dtc, a_c, acs_c = load_chunk(c)
        ecs_c = acs_c - a_c
        total = acs_c[L - 1:L, :]
        state = state_ref[...]
        y_off = (jnp.dot(cc, state.astype(_BF16), preferred_element_type=_F32)
                 * spread(jnp.exp(total - ecs_c), expand_b))
        y = y_ref[pl.ds(r0, L), :] + y_off + dskip_ref[...] * xs.astype(_F32)
        gated = y * _silu(z_ref[pl.ds(r0, L), :].astype(_F32))
        o_ref[pl.ds(r0, L), :] = _rms_rows(gated, nw_ref[...]).astype(_BF16)
        state_ref[...] = state_step(state, bc, xs, jnp.exp(ecs_c) * dtc, total, expand_b)
        return carry

    lax.fori_loop(0, n_chunks, bwd_body, 0, unroll=2)


def _ssd_mixer(z, xbc, dt, dtt, conv_w, conv_b, dt_bias, a_log, d_skip, norm_w, batch, seq):
    n_chunks = seq // SSD_CHUNK
    per_b = lambda w: pl.BlockSpec((None, seq, w), lambda b: (b, 0, 0))
    dtb = dt_bias.reshape(DT_COLS).astype(_F32)
    alog = a_log.reshape(DT_COLS).astype(_F32)
    out = pl.pallas_call(
        _ssd_kernel,
        grid=(batch,),
        in_specs=[per_b(SSD_CONV_CH), per_b(SSD_WIDTH), per_b(DT_COLS),
                  pl.BlockSpec((n_chunks, DT_COLS, SSD_CHUNK), lambda b: (b, 0, 0)),
                  _full((SSD_CONV, SSD_CONV_CH)), _full((1, SSD_CONV_CH)),
                  _full((1, DT_COLS)), _full((DT_COLS, 1)), _full((1, DT_COLS)), _full((DT_COLS, 1)),
                  _full((1, SSD_WIDTH)), _full((1, SSD_WIDTH))],
        out_specs=per_b(SSD_WIDTH),
        out_shape=jax.ShapeDtypeStruct((batch, seq, SSD_WIDTH), _BF16),
        scratch_shapes=[pltpu.VMEM((seq + 16, SSD_CONV_CH), _F32), pltpu.VMEM((seq, SSD_CONV_CH), _BF16),
                        pltpu.VMEM((seq, SSD_WIDTH), _F32), pltpu.VMEM((SSD_BC, SSD_WIDTH), _F32)],
        compiler_params=_params("parallel"),
        name="ssd_mixer",
    )(xbc.reshape(batch, seq, SSD_CONV_CH), z.reshape(batch, seq, SSD_WIDTH), dt.reshape(batch, seq, DT_COLS),
      dtt, conv_w.astype(_F32), conv_b.reshape(1, SSD_CONV_CH).astype(_F32),
      dtb.reshape(1, DT_COLS), dtb.reshape(DT_COLS, 1), alog.reshape(1, DT_COLS), alog.reshape(DT_COLS, 1),
      jnp.repeat(d_skip.astype(_F32), SSD_HEAD_DIM).reshape(1, SSD_WIDTH), norm_w.reshape(1, SSD_WIDTH))
    return out.reshape(batch * seq, SSD_WIDTH)


def _proj_router_kernel(*refs, n_act):
    x_ref = refs[0]
    acts = refs[1:1 + n_act]
    ws = refs[1 + n_act:1 + 2 * n_act]
    nw_ref, wr_hi_ref, wr_lo_ref, br_ref, x1_ref, h_ref, route_ref, gate_ref, count_ref = refs[1 + 2 * n_act:]
    tm = x_ref.shape[0]
    x1 = x_ref[...]
    for a_ref, w_ref in zip(acts, ws):
        x1 = x1 + jnp.dot(a_ref[...], w_ref[...], preferred_element_type=_F32)
    x1_ref[...] = x1
    h = _rms_rows(x1, nw_ref[...])
    _store_row_tiles(h_ref, h)

    h_hi = h.astype(_BF16)
    h_lo = (h - h_hi.astype(_F32)).astype(_BF16)
    logits = (jnp.dot(h_hi, wr_hi_ref[...], preferred_element_type=_F32)
              + jnp.dot(h_lo, wr_hi_ref[...], preferred_element_type=_F32)
              + jnp.dot(h_hi, wr_lo_ref[...], preferred_element_type=_F32) + br_ref[...])
    lane = lax.broadcasted_iota(jnp.int32, (tm, V7X_LANES), 1)
    lane_f = lane.astype(_F32)
    ninf = jnp.float32(-jnp.inf)
    nolane = jnp.float32(V7X_LANES)

    def argmax_first(vals):
        m = jnp.max(vals, axis=-1, keepdims=True)
        idx = jnp.min(jnp.where(vals == m, lane_f, nolane), axis=-1, keepdims=True)
        return m, idx

    gl = jnp.where(lane < MOE_GROUPS, logits, ninf)
    gmax, gidx = argmax_first(gl)
    g_p = 1.0 / jnp.sum(jnp.exp(gl - gmax), axis=-1, keepdims=True)
    first = MOE_GROUPS + MOE_EXPERTS_PER_GROUP * gidx
    el = jnp.where((lane_f >= first) & (lane_f < first + MOE_EXPERTS_PER_GROUP), logits, ninf)
    e1, i1 = argmax_first(el)
    e2, i2 = argmax_first(jnp.where(lane_f == i1, ninf, el))
    r = jnp.exp(e2 - e1)
    p1 = 1.0 / (1.0 + r)
    gate_ref[...] = jnp.where(lane == 0, g_p * p1, jnp.where(lane == 1, g_p * (r * p1), 0.0))

    @pl.when(pl.program_id(0) == 0)
    def _():
        count_ref[...] = jnp.zeros_like(count_ref)

    chosen = (lane_f == i1) | (lane_f == i2)
    onehot = jnp.where(chosen, 1.0, 0.0)
    earlier = (lax.broadcasted_iota(jnp.int32, (tm, tm), 1)
               < lax.broadcasted_iota(jnp.int32, (tm, tm), 0)).astype(_BF16)
    prior = jnp.dot(earlier, onehot.astype(_BF16), preferred_element_type=_F32) + count_ref[0:1, :]
    rank1 = jnp.sum(jnp.where(lane_f == i1, prior, 0.0), axis=-1, keepdims=True)
    rank2 = jnp.sum(jnp.where(lane_f == i2, prior, 0.0), axis=-1, keepdims=True)
    count_ref[...] = count_ref[...] + jnp.sum(onehot, axis=0, keepdims=True)
    route = jnp.where(lane == 0, i1 - MOE_GROUPS, jnp.where(lane == 1, i2 - MOE_GROUPS,
                      jnp.where(lane == 2, rank1, jnp.where(lane == 3, rank2, 0.0))))
    route_ref[...] = route.astype(jnp.int32)


def _proj_router(x2d, acts, weights, norm_w, rg_w, rg_b, re_w, re_b):
    t = x2d.shape[0]
    tm = _tile(t, TOKEN_TILE)
    n_act = len(acts)
    n_log = MOE_GROUPS + MOE_EXPERTS
    wr = jnp.zeros((D_MODEL, V7X_LANES), _F32).at[:, :n_log].set(
        jnp.concatenate([rg_w.astype(_F32), re_w.reshape(D_MODEL, MOE_EXPERTS).astype(_F32)], axis=1))
    br = jnp.zeros((1, V7X_LANES), _F32).at[0, :n_log].set(
        jnp.concatenate([rg_b.astype(_F32), re_b.reshape(MOE_EXPERTS).astype(_F32)]))
    wr_hi = wr.astype(_BF16)
    wr_lo = (wr - wr_hi.astype(_F32)).astype(_BF16)
    row = lambda i: (i, 0)
    x1, h, route, gate, counts = pl.pallas_call(
        functools.partial(_proj_router_kernel, n_act=n_act),
        grid=(t // tm,),
        in_specs=[pl.BlockSpec((tm, D_MODEL), row)]
                 + [pl.BlockSpec((tm, a.shape[1]), row) for a in acts]
                 + [_full(w.shape) for w in weights]
                 + [_full((1, D_MODEL)), _full((D_MODEL, V7X_LANES)), _full((D_MODEL, V7X_LANES)),
                    _full((1, V7X_LANES))],
        out_specs=[pl.BlockSpec((tm, D_MODEL), row), pl.BlockSpec((tm * ROW_SUBLANES, V7X_LANES), row),
                   pl.BlockSpec((tm, V7X_LANES), row), pl.BlockSpec((tm, V7X_LANES), row),
                   _full((V7X_SUBLANES, V7X_LANES))],
        out_shape=[jax.ShapeDtypeStruct((t, D_MODEL), _F32),
                   jax.ShapeDtypeStruct((t * ROW_SUBLANES, V7X_LANES), _F32),
                   jax.ShapeDtypeStruct((t, V7X_LANES), jnp.int32), jax.ShapeDtypeStruct((t, V7X_LANES), _F32),
                   jax.ShapeDtypeStruct((V7X_SUBLANES, V7X_LANES), _F32)],
        compiler_params=_params("arbitrary"),
        name="proj_router",
    )(x2d, *acts, *[w.astype(_BF16) for w in weights], norm_w.reshape(1, D_MODEL), wr_hi, wr_lo, br)
    counts = counts[0, MOE_GROUPS:MOE_GROUPS + MOE_EXPERTS].astype(jnp.int32)
    return x1, h, route, gate, counts


def _store_row_tiles(ref, val):
    n = val.shape[0]
    for s in range(ROW_SUBLANES):
        ref[pl.ds(s, n, stride=ROW_SUBLANES), :] = val[:, s * V7X_LANES:(s + 1) * V7X_LANES]


def _load_row_tiles(ref, n):
    return jnp.concatenate([ref[pl.ds(s, n, stride=ROW_SUBLANES), :] for s in range(ROW_SUBLANES)], axis=1)


def _expert_kernel(bexp_ref, nused_ref, gidx_ref, gnext_ref, sidx_ref, h_hbm, w1_ref, w3_ref, w2_ref, out_hbm,
                   xbuf, ybuf, w13_bf, w2_bf, sem_in, sem_out):
    i = pl.program_id(0)

    @pl.when((i == 0) | (bexp_ref[i] != bexp_ref[jnp.maximum(i - 1, 0)]))
    def _():
        w13_bf[:, :MOE_HIDDEN] = w1_ref[...].astype(_BF16)
        w13_bf[:, MOE_HIDDEN:] = w3_ref[...].astype(_BF16)
        w2_bf[...] = w2_ref[...].astype(_BF16)

    n_used = nused_ref[0]
    slot = lax.rem(i, 2)
    other = 1 - slot
    tile = ROW_SUBLANES

    def gather_copy(buf_slot, r, tok):
        return pltpu.make_async_copy(h_hbm.at[pl.ds(pl.multiple_of(tok * tile, tile), tile)],
                                     xbuf.at[buf_slot, pl.ds(pl.multiple_of(r * tile, tile), tile)],
                                     sem_in.at[buf_slot])

    def scatter_copy(buf_slot, r, row):
        return pltpu.make_async_copy(ybuf.at[buf_slot, pl.ds(pl.multiple_of(r * tile, tile), tile)],
                                     out_hbm.at[pl.ds(pl.multiple_of(row * tile, tile), tile)],
                                     sem_out.at[buf_slot])

    def for_rows(fn):
        def body(j, carry):
            for u in range(DMA_UNROLL):
                fn(j * DMA_UNROLL + u)
            return carry
        lax.fori_loop(0, MOE_ROWS // DMA_UNROLL, body, 0)

    def start_gather(idx_ref, buf_slot):
        for_rows(lambda r: gather_copy(buf_slot, r, idx_ref[0, r]).start())

    def wait_gather(buf_slot):
        for r in range(MOE_ROWS):
            gather_copy(buf_slot, r, 0).wait()

    def start_scatter(buf_slot):
        for_rows(lambda r: scatter_copy(buf_slot, r, sidx_ref[0, r]).start())

    def wait_scatter(buf_slot):
        for r in range(MOE_ROWS):
            scatter_copy(buf_slot, r, 0).wait()

    @pl.when(i == 0)
    def _():
        spare0 = out_hbm.shape[0] - 2 * MOE_ROWS * tile
        ybuf[...] = jnp.zeros_like(ybuf)
        fills = [pltpu.make_async_copy(ybuf.at[s], out_hbm.at[pl.ds(spare0 + s * MOE_ROWS * tile, MOE_ROWS * tile)],
                                       sem_out.at[s]) for s in range(2)]
        for f in fills:
            f.start()
        for f in fills:
            f.wait()
        start_gather(gidx_ref, 0)

    @pl.when(i + 1 < n_used)
    def _():
        start_gather(gnext_ref, other)

    @pl.when(i < n_used)
    def _():
        wait_gather(slot)
        xb = _load_row_tiles(xbuf.at[slot], MOE_ROWS).astype(_BF16)
        a = jnp.dot(xb, w13_bf[...], preferred_element_type=_F32)
        hid = _silu(a[:, :MOE_HIDDEN]) * a[:, MOE_HIDDEN:]
        y = jnp.dot(hid.astype(_BF16), w2_bf[...], preferred_element_type=_F32)

        @pl.when(i >= 2)
        def _():
            wait_scatter(slot)

        _store_row_tiles(ybuf.at[slot], y)
        start_scatter(slot)

        @pl.when(i == n_used - 1)
        def _():
            wait_scatter(slot)

            @pl.when(i >= 1)
            def _():
                wait_scatter(other)


def _expert_plan(route, counts, n_tok):
    n_assign = n_tok * MOE_TOP_K
    n_blocks = n_assign // MOE_ROWS + MOE_EXPERTS
    n_slots = n_blocks * MOE_ROWS
    experts = jnp.arange(MOE_EXPERTS, dtype=jnp.int32)
    nblk = (counts + MOE_ROWS - 1) // MOE_ROWS
    blk_end = jnp.cumsum(nblk)
    blk_first = blk_end - nblk
    n_used = blk_end[-1]
    bid = jnp.arange(n_blocks, dtype=jnp.int32)
    bexp = jnp.minimum(jnp.sum(bid[:, None] >= blk_end[None, :], axis=1), MOE_EXPERTS - 1).astype(jnp.int32)
    last_exp = jnp.max(jnp.where(counts > 0, experts, 0))
    bexp = jnp.where(bid < n_used, bexp, last_exp)
    e = route[:, 0:MOE_TOP_K]
    rank = route[:, MOE_TOP_K:2 * MOE_TOP_K]
    first_slot = jnp.sum(jnp.where(e[:, :, None] == experts, blk_first * MOE_ROWS, 0), axis=-1)
    dest = (first_slot + rank).reshape(n_assign)
    tok = jnp.broadcast_to(jnp.arange(n_tok, dtype=jnp.int32)[:, None], (n_tok, MOE_TOP_K))
    row = tok + jnp.arange(MOE_TOP_K, dtype=jnp.int32) * n_tok
    r = jnp.arange(MOE_ROWS, dtype=jnp.int32)
    spare = (MOE_TOP_K * n_tok + (bid[:, None] % 2) * MOE_ROWS + r[None, :]).reshape(n_slots)
    gidx = jnp.zeros((n_slots,), jnp.int32).at[dest].set(tok.reshape(n_assign), unique_indices=True)
    sidx = spare.at[dest].set(row.reshape(n_assign), unique_indices=True)
    shape3 = (n_blocks, 1, MOE_ROWS)
    return bexp, n_used.reshape(1).astype(jnp.int32), gidx.reshape(shape3), sidx.reshape(shape3), n_blocks


def _experts(h_tiles, route, counts, w1, w3, w2, layer):
    n_tok = h_tiles.shape[0] // ROW_SUBLANES
    bexp, n_used, gidx, sidx, n_blocks = _expert_plan(route, counts, n_tok)
    idx_block = (None, 1, MOE_ROWS)
    per_expert = lambda i, bexp, nu: (layer, bexp[i], 0, 0)
    grid_spec = pltpu.PrefetchScalarGridSpec(
        num_scalar_prefetch=2,
        grid=(n_blocks,),
        in_specs=[pl.BlockSpec(idx_block, lambda i, bexp, nu: (i, 0, 0), memory_space=pltpu.SMEM),
                  pl.BlockSpec(idx_block, lambda i, bexp, nu: (jnp.minimum(i + 1, n_blocks - 1), 0, 0),
                               memory_space=pltpu.SMEM),
                  pl.BlockSpec(idx_block, lambda i, bexp, nu: (i, 0, 0), memory_space=pltpu.SMEM),
                  pl.BlockSpec(memory_space=pl.ANY),
                  pl.BlockSpec((None, None, D_MODEL, MOE_HIDDEN), per_expert),
                  pl.BlockSpec((None, None, D_MODEL, MOE_HIDDEN), per_expert),
                  pl.BlockSpec((None, None, MOE_HIDDEN, D_MODEL), per_expert)],
        out_specs=pl.BlockSpec(memory_space=pl.ANY),
        scratch_shapes=[pltpu.VMEM((2, MOE_ROWS * ROW_SUBLANES, V7X_LANES), _F32),
                        pltpu.VMEM((2, MOE_ROWS * ROW_SUBLANES, V7X_LANES), _F32),
                        pltpu.VMEM((D_MODEL, 2 * MOE_HIDDEN), _BF16), pltpu.VMEM((MOE_HIDDEN, D_MODEL), _BF16),
                        pltpu.SemaphoreType.DMA((2,)), pltpu.SemaphoreType.DMA((2,))],
    )
    out_rows = MOE_TOP_K * n_tok + 2 * MOE_ROWS
    return pl.pallas_call(
        _expert_kernel,
        grid_spec=grid_spec,
        out_shape=jax.ShapeDtypeStruct((out_rows * ROW_SUBLANES, V7X_LANES), _F32),
        compiler_params=_params("arbitrary"),
        name="experts",
    )(bexp, n_used, gidx, gidx, sidx, h_tiles, w1, w3, w2)


def _combine(x1_ref, y0_ref, y1_ref, gate_ref):
    gate = gate_ref[...]
    tm = x1_ref.shape[0]
    return x1_ref[...] + gate[:, 0:1] * _load_row_tiles(y0_ref, tm) + gate[:, 1:2] * _load_row_tiles(y1_ref, tm)


def _combine_kernel(x1_ref, y0_ref, y1_ref, gate_ref, o_ref):
    o_ref[...] = _combine(x1_ref, y0_ref, y1_ref, gate_ref)


def _combine_dft_kernel(x1_ref, y0_ref, y1_ref, gate_ref, nw_ref, cosc_ref, sinc_ref, x_ref, hc_ref, hs_ref):
    x = _combine(x1_ref, y0_ref, y1_ref, gate_ref)
    x_ref[...] = x
    h = _rms_rows(x, nw_ref[...]).astype(_BF16)
    for g in range(FOURIER_GROUPS):
        c0 = g * FOURIER_GROUP_DIM
        hg = h[:, c0:c0 + FOURIER_GROUP_DIM]
        hc_ref[:, c0:c0 + FOURIER_GROUP_DIM] = jnp.dot(hg, cosc_ref[...], preferred_element_type=_F32).astype(_BF16)
        hs_ref[:, c0:c0 + FOURIER_GROUP_DIM] = jnp.dot(hg, sinc_ref[...], preferred_element_type=_F32).astype(_BF16)


def _moe_combine(x1, y, gate, dft=None):
    t = x1.shape[0]
    tm = _tile(t, TOKEN_TILE)
    row = lambda i: (i, 0)
    second = lambda i: (i + t // tm, 0)
    y_block = (tm * ROW_SUBLANES, V7X_LANES)
    in_specs = [pl.BlockSpec((tm, D_MODEL), row), pl.BlockSpec(y_block, row),
                pl.BlockSpec(y_block, second), pl.BlockSpec((tm, V7X_LANES), row)]
    if dft is None:
        return pl.pallas_call(
            _combine_kernel, grid=(t // tm,), in_specs=in_specs,
            out_specs=pl.BlockSpec((tm, D_MODEL), row),
            out_shape=jax.ShapeDtypeStruct((t, D_MODEL), _F32),
            compiler_params=_params("parallel"), name="moe_combine",
        )(x1, y, y, gate)
    norm_w, cosc, sinc = dft
    return pl.pallas_call(
        _combine_dft_kernel, grid=(t // tm,),
        in_specs=in_specs + [_full((1, D_MODEL)), _full(cosc.shape), _full(sinc.shape)],
        out_specs=[pl.BlockSpec((tm, D_MODEL), row)] * 3,
        out_shape=[jax.ShapeDtypeStruct((t, D_MODEL), _F32), jax.ShapeDtypeStruct((t, D_MODEL), _BF16),
                   jax.ShapeDtypeStruct((t, D_MODEL), _BF16)],
        compiler_params=_params("parallel"), name="moe_combine_dft",
    )(x1, y, y, gate, norm_w.reshape(1, D_MODEL), cosc, sinc)


def _seq_dft_kernel(ca_ref, sa_ref, cb_ref, sb_ref, hc_ref, hs_ref, o_ref, cos_tab, nsin_tab):
    tq = cos_tab.shape[0]

    @pl.when(pl.program_id(1) == 0)
    def _():
        cb = cb_ref[...]
        sb = sb_ref[...]
        for g in range(tq // DFT_FINE):
            j1 = pl.program_id(0) * (tq // DFT_FINE) + g
            ca = ca_ref[pl.ds(j1, 1), :]
            sa = sa_ref[pl.ds(j1, 1), :]
            cos_tab[g * DFT_FINE:(g + 1) * DFT_FINE, :] = (ca * cb - sa * sb).astype(_BF16)
            nsin_tab[g * DFT_FINE:(g + 1) * DFT_FINE, :] = (-(sa * cb + ca * sb)).astype(_BF16)

    o_ref[...] = (jnp.dot(cos_tab[...], hc_ref[...], preferred_element_type=_F32)
                  + jnp.dot(nsin_tab[...], hs_ref[...], preferred_element_type=_F32)).astype(_BF16)


def _seq_dft(hc, hs, batch, seq):
    tq = _tile(seq, DFT_ROW_TILE)
    assert tq % DFT_FINE == 0
    n_coarse = seq // DFT_FINE
    ca, sa = _dft_tables(jnp.arange(n_coarse, dtype=jnp.int32) * DFT_FINE, seq, seq ** -0.5)
    cb, sb = _dft_tables(jnp.arange(DFT_FINE, dtype=jnp.int32), seq, 1.0)
    per_b = pl.BlockSpec((None, seq, D_MODEL), lambda i, b: (b, 0, 0))
    out = pl.pallas_call(
        _seq_dft_kernel, grid=(seq // tq, batch),
        in_specs=[_full((n_coarse, seq)), _full((n_coarse, seq)), _full((DFT_FINE, seq)), _full((DFT_FINE, seq)),
                  per_b, per_b],
        out_specs=pl.BlockSpec((None, tq, D_MODEL), lambda i, b: (b, i, 0)),
        out_shape=jax.ShapeDtypeStruct((batch, seq, D_MODEL), _BF16),
        scratch_shapes=[pltpu.VMEM((tq, seq), _BF16), pltpu.VMEM((tq, seq), _BF16)],
        compiler_params=_params("parallel", "arbitrary"), name="seq_dft",
    )(ca, sa, cb, sb, hc.reshape(batch, seq, D_MODEL), hs.reshape(batch, seq, D_MODEL))
    return out.reshape(batch * seq, D_MODEL)


def _dft_tables(rows, n, scale):
    k = jnp.arange(n, dtype=jnp.int32)
    m = (rows[:, None] * k[None, :]) % n
    ang = m.astype(_F32) * (2.0 * math.pi / n)
    return jnp.cos(ang) * scale, jnp.sin(ang) * scale


def kernel(x, norm_mix, norm_ffn, w_in, q_norm, k_norm, lambda_q1, lambda_k1, lambda_q2, lambda_k2,
           attn_subln, conv_w, conv_b, dt_bias, a_log, d_skip, ssd_norm, w_out, w_fourier,
           router_group_w, router_group_b, router_expert_w, router_expert_b,
           expert_w1, expert_w3, expert_w2):
    batch, seq, _ = x.shape
    depth = norm_mix.shape[0]
    xt = x.reshape(batch * seq, D_MODEL)
    pending = None
    for l in range(depth):
        i = l // 2
        if l % 2 == 0:
            if pending is not None:
                xt = _moe_combine(*pending)
            lambda_init = 0.8 - 0.6 * math.exp(-0.3 * l)
            q, k, v, z, xbc, dt, dtt = _in_proj(xt, norm_mix[l], w_in[i], q_norm[i], k_norm[i])
            attn = _diff_attention(q, k, v, lambda_q1[i], lambda_k1[i], lambda_q2[i], lambda_k2[i],
                                   attn_subln[i], lambda_init, batch, seq)
            ssd = _ssd_mixer(z, xbc, dt, dtt, conv_w[i], conv_b[i], dt_bias[i], a_log[i], d_skip[i],
                             ssd_norm[i], batch, seq)
            acts = [attn, ssd]
            weights = [w_out[i][:ATTN_WIDTH], w_out[i][ATTN_WIDTH:]]
        else:
            cos_c, sin_c = _dft_tables(jnp.arange(FOURIER_GROUP_DIM, dtype=jnp.int32), FOURIER_GROUP_DIM,
                                       FOURIER_GROUP_DIM ** -0.5)
            dft = (norm_mix[l], cos_c.astype(_BF16), sin_c.astype(_BF16))
            if pending is None:
                zeros = jnp.zeros((2 * batch * seq * ROW_SUBLANES, V7X_LANES), _F32)
                pending = (xt, zeros, jnp.zeros((batch * seq, V7X_LANES), _F32))
            xt, hc, hs = _moe_combine(*pending, dft=dft)
            f = _seq_dft(hc, hs, batch, seq)
            acts = [f]
            weights = [w_fourier[i]]
        x1, h, route, gate, counts = _proj_router(xt, acts, weights, norm_ffn[l], router_group_w[l],
                                                  router_group_b[l], router_expert_w[l], router_expert_b[l])
        y = _experts(h, route, counts, expert_w1, expert_w3, expert_w2, l)
        pending = (x1, y, gate)
    xt = _moe_combine(*pending)
    return xt.reshape(batch, seq, D_MODEL)
```

```python
import functools
import math

import jax
import jax.numpy as jnp
from jax import lax
from jax.experimental import pallas as pl
from jax.experimental.pallas import tpu as pltpu

D_MODEL = 1024
N_DIFF_HEADS = 4
DIFF_QK_DIM = 64
DIFF_V_DIM = 2 * DIFF_QK_DIM
ATTN_WIDTH = N_DIFF_HEADS * DIFF_V_DIM
SSD_HEAD_DIM = 64
SSD_WIDTH = D_MODEL // 2
SSD_HEADS = SSD_WIDTH // SSD_HEAD_DIM
SSD_GROUPS = 2
SSD_HEADS_PER_GROUP = SSD_HEADS // SSD_GROUPS
SSD_STATE = 64
SSD_CONV = 5
SSD_CHUNK = 128
SSD_BC = SSD_GROUPS * SSD_STATE
SSD_CONV_CH = SSD_WIDTH + 2 * SSD_BC
Q_COLS = N_DIFF_HEADS * 2 * DIFF_QK_DIM
DT_COLS = 2 * SSD_HEADS
FOURIER_GROUPS = 4
FOURIER_GROUP_DIM = D_MODEL // FOURIER_GROUPS
MOE_GROUPS = 4
MOE_EXPERTS_PER_GROUP = 8
MOE_EXPERTS = MOE_GROUPS * MOE_EXPERTS_PER_GROUP
MOE_TOP_K = 2
MOE_HIDDEN = 256
RMS_EPS = 1e-6
LOG2_E = math.log2(math.e)

V7X_LANES = 128
V7X_VMEM_LIMIT = 48 * 1024 * 1024
TOKEN_TILE = 512
ATTN_Q_TILE = 256
DFT_ROW_TILE = 512
MOE_ROWS = 256
V7X_SUBLANES = 8
ROW_SUBLANES = D_MODEL // V7X_LANES
DFT_FINE = 64
DMA_UNROLL = 16
CONV_ROWS = 128

_F32 = jnp.float32
_BF16 = jnp.bfloat16
_NT = (((1,), (1,)), ((), ()))
_TN = (((0,), (0,)), ((), ()))


def _params(*sem):
    return pltpu.CompilerParams(dimension_semantics=sem, vmem_limit_bytes=V7X_VMEM_LIMIT)


def _tile(n, want):
    t = min(n, want)
    assert n % t == 0, (n, t)
    return t


def _full(shape):
    return pl.BlockSpec(shape, lambda *_: (0,) * len(shape))


def _rms_rows(x, w):
    ms = jnp.mean(x * x, axis=-1, keepdims=True)
    return x * lax.rsqrt(ms + RMS_EPS) * w


def _silu(x):
    return x * jax.nn.sigmoid(x)


def _softplus(x):
    return jnp.maximum(x, 0.0) + jnp.log1p(jnp.exp(-jnp.abs(x)))


def _in_proj_kernel(x_ref, nw_ref, wqkv_ref, wz_ref, wxbc_ref, wdt_ref, wdtt_ref, qn_ref, kn_ref,
                    q_ref, k_ref, v_ref, z_ref, xbc_ref, dt_ref, dtt_ref):
    tm = x_ref.shape[0]
    h = _rms_rows(x_ref[...], nw_ref[...]).astype(_BF16)
    qkv = jnp.dot(h, wqkv_ref[...], preferred_element_type=_F32)
    low = lax.broadcasted_iota(jnp.int32, (tm, V7X_LANES), 1) < DIFF_QK_DIM

    def qk_norm(slab, w):
        sq = slab * slab
        s_lo = jnp.sum(jnp.where(low, sq, 0.0), axis=-1, keepdims=True)
        s_hi = jnp.sum(jnp.where(low, 0.0, sq), axis=-1, keepdims=True)
        ms = jnp.where(low, s_lo, s_hi) * (1.0 / DIFF_QK_DIM)
        return slab * lax.rsqrt(ms + RMS_EPS) * w

    scale = DIFF_QK_DIM ** -0.5 * LOG2_E
    for hd in range(N_DIFF_HEADS):
        c0 = hd * DIFF_V_DIM
        q_ref[:, c0:c0 + DIFF_V_DIM] = (qk_norm(qkv[:, c0:c0 + DIFF_V_DIM], qn_ref[...]) * scale).astype(_BF16)
        k_ref[:, c0:c0 + DIFF_V_DIM] = qk_norm(qkv[:, Q_COLS + c0:Q_COLS + c0 + DIFF_V_DIM], kn_ref[...]).astype(_BF16)
    v_ref[...] = qkv[:, 2 * Q_COLS:].astype(_BF16)
    z_ref[...] = jnp.dot(h, wz_ref[...], preferred_element_type=_F32).astype(_BF16)
    xbc_ref[...] = jnp.dot(h, wxbc_ref[...], preferred_element_type=_F32)
    dt_ref[...] = jnp.dot(h, wdt_ref[...], preferred_element_type=_F32)
    dtt = lax.dot_general(wdtt_ref[...], h, _NT, preferred_element_type=_F32)
    for j in range(tm // SSD_CHUNK):
        dtt_ref[j] = dtt[:, j * SSD_CHUNK:(j + 1) * SSD_CHUNK]


def _in_proj(x2d, norm_w, w_in, q_norm, k_norm):
    t = x2d.shape[0]
    tm = _tile(t, TOKEN_TILE)
    c_qkv = 3 * Q_COLS
    c_z = c_qkv + SSD_WIDTH
    c_xbc = c_z + SSD_CONV_CH
    wb = w_in.astype(_BF16)
    wqkv, wz, wxbc, wdt = wb[:, :c_qkv], wb[:, c_qkv:c_z], wb[:, c_z:c_xbc], wb[:, c_xbc:]
    row = lambda i: (i, 0)
    outs = pl.pallas_call(
        _in_proj_kernel,
        grid=(t // tm,),
        in_specs=[pl.BlockSpec((tm, D_MODEL), row), _full((1, D_MODEL)),
                  _full((D_MODEL, c_qkv)), _full((D_MODEL, SSD_WIDTH)), _full((D_MODEL, SSD_CONV_CH)),
                  _full((D_MODEL, DT_COLS)), _full((DT_COLS, D_MODEL)),
                  _full((1, DIFF_V_DIM)), _full((1, DIFF_V_DIM))],
        out_specs=[pl.BlockSpec((tm, Q_COLS), row), pl.BlockSpec((tm, Q_COLS), row),
                   pl.BlockSpec((tm, ATTN_WIDTH), row), pl.BlockSpec((tm, SSD_WIDTH), row),
                   pl.BlockSpec((tm, SSD_CONV_CH), row), pl.BlockSpec((tm, DT_COLS), row),
                   pl.BlockSpec((tm // SSD_CHUNK, DT_COLS, SSD_CHUNK), lambda i: (i, 0, 0))],
        out_shape=[jax.ShapeDtypeStruct((t, Q_COLS), _BF16), jax.ShapeDtypeStruct((t, Q_COLS), _BF16),
                   jax.ShapeDtypeStruct((t, ATTN_WIDTH), _BF16), jax.ShapeDtypeStruct((t, SSD_WIDTH), _BF16),
                   jax.ShapeDtypeStruct((t, SSD_CONV_CH), _F32), jax.ShapeDtypeStruct((t, DT_COLS), _F32),
                   jax.ShapeDtypeStruct((t // SSD_CHUNK, DT_COLS, SSD_CHUNK), _F32)],
        compiler_params=_params("parallel"),
        name="in_proj",
    )(x2d, norm_w.reshape(1, D_MODEL), wqkv, wz, wxbc, wdt, wdt.T,
      jnp.tile(q_norm, 2).reshape(1, DIFF_V_DIM), jnp.tile(k_norm, 2).reshape(1, DIFF_V_DIM))
    return outs


def _attn_kernel(slope_ref, lq1_ref, lk1_ref, lq2_ref, lk2_ref, q_ref, k_ref, v_ref, subln_ref, o_ref,
                 bias_ref, *, lambda_init):
    tq = q_ref.shape[0]
    seq = k_ref.shape[0]
    hd = pl.program_id(0)
    qi = pl.program_id(1)

    @pl.when(pl.program_id(2) == 0)
    def _():
        row = qi * tq + lax.broadcasted_iota(jnp.int32, (tq, seq), 0)
        col = lax.broadcasted_iota(jnp.int32, (tq, seq), 1)
        bias_ref[...] = (slope_ref[hd] * LOG2_E) * jnp.abs(row - col).astype(_F32)

    lam = (jnp.exp(jnp.sum(lq1_ref[...] * lk1_ref[...], keepdims=True))
           - jnp.exp(jnp.sum(lq2_ref[...] * lk2_ref[...], keepdims=True)) + lambda_init)
    q = q_ref[...]
    k = k_ref[...]
    low = lax.broadcasted_iota(jnp.int32, (tq, V7X_LANES), 1) < DIFF_QK_DIM
    zero = jnp.zeros_like(q)

    def branch(qc):
        s = lax.dot_general(qc, k, _NT, preferred_element_type=_F32) - bias_ref[...]
        p = jnp.exp2(s - jnp.max(s, axis=-1, keepdims=True))
        return p, jnp.sum(p, axis=-1, keepdims=True)

    p1, l1 = branch(jnp.where(low, q, zero))
    p2, l2 = branch(jnp.where(low, zero, q))
    w = p1 * (1.0 / l1) - p2 * (lam / l2)
    o = jnp.dot(w.astype(_BF16), v_ref[...], preferred_element_type=_F32)
    o_ref[...] = (_rms_rows(o, subln_ref[...]) * (1.0 - lambda_init)).astype(_BF16)


def _diff_attention(q, k, v, lq1, lk1, lq2, lk2, subln, lambda_init, batch, seq):
    tq = _tile(seq, ATTN_Q_TILE)
    slopes = jnp.asarray([2.0 ** (-8.0 * (h + 1) / N_DIFF_HEADS) for h in range(N_DIFF_HEADS)], _F32)
    vec = lambda a: a.reshape(1, -1).astype(_F32)
    q3, k3, v3 = (a.reshape(batch, seq, ATTN_WIDTH) for a in (q, k, v))
    kv_spec = pl.BlockSpec((None, seq, DIFF_V_DIM), lambda h, i, b: (b, 0, h))
    q_spec = pl.BlockSpec((None, tq, DIFF_V_DIM), lambda h, i, b: (b, i, h))
    out = pl.pallas_call(
        functools.partial(_attn_kernel, lambda_init=lambda_init),
        grid=(N_DIFF_HEADS, seq // tq, batch),
        in_specs=[pl.BlockSpec(memory_space=pltpu.SMEM)] + [_full((1, DIFF_QK_DIM))] * 4
                 + [q_spec, kv_spec, kv_spec, _full((1, DIFF_V_DIM))],
        out_specs=q_spec,
        out_shape=jax.ShapeDtypeStruct((batch, seq, ATTN_WIDTH), _BF16),
        scratch_shapes=[pltpu.VMEM((tq, seq), _F32)],
        compiler_params=_params("parallel", "parallel", "arbitrary"),
        name="diff_attn",
    )(slopes, vec(lq1), vec(lk1), vec(lq2), vec(lk2), q3, k3, v3, vec(subln))
    return out.reshape(batch * seq, ATTN_WIDTH)


def _ssd_kernel(xbc_ref, z_ref, dt_ref, dtt_ref, cw_ref, cb_ref, dtb_row_ref, dtb_col_ref,
                alog_row_ref, alog_col_ref, dskip_ref, nw_ref, o_ref,
                xpad_ref, xc_ref, y_ref, state_ref):
    seq = xbc_ref.shape[0]
    n_chunks = seq // SSD_CHUNK
    L = SSD_CHUNK
    halo = 8
    pad = SSD_CONV // 2

    xpad_ref[0:halo, :] = jnp.zeros((halo, SSD_CONV_CH), _F32)
    xpad_ref[halo + seq:2 * halo + seq, :] = jnp.zeros((halo, SSD_CONV_CH), _F32)
    xpad_ref[halo:halo + seq, :] = xbc_ref[...]
    for rb in range(seq // CONV_ROWS):
        base = rb * CONV_ROWS + halo - pad
        acc = jnp.broadcast_to(cb_ref[...], (CONV_ROWS, SSD_CONV_CH))
        for j in range(SSD_CONV):
            acc = acc + cw_ref[j:j + 1, :] * xpad_ref[base + j:base + j + CONV_ROWS, :]
        xc_ref[rb * CONV_ROWS:(rb + 1) * CONV_ROWS, :] = _silu(acc).astype(_BF16)

    sub = lax.broadcasted_iota(jnp.int32, (L, L), 0)
    lane = lax.broadcasted_iota(jnp.int32, (L, L), 1)
    causal = lane <= sub
    anti = lane >= sub
    tri = causal.astype(_BF16)
    trit = anti.astype(_BF16)
    neg_big = jnp.float32(-1e30)
    head_of_col = lax.broadcasted_iota(jnp.int32, (DT_COLS, SSD_WIDTH), 1) // SSD_HEAD_DIM
    dtcol = lax.broadcasted_iota(jnp.int32, (DT_COLS, SSD_WIDTH), 0)
    expand_f = (dtcol == head_of_col).astype(_BF16)
    expand_b = (dtcol == head_of_col + SSD_HEADS).astype(_BF16)
    state_row_group = lax.broadcasted_iota(jnp.int32, (SSD_BC, SSD_WIDTH), 0) // SSD_STATE
    state_col_group = (lax.broadcasted_iota(jnp.int32, (SSD_BC, SSD_WIDTH), 1)
                       // (SSD_HEAD_DIM * SSD_HEADS_PER_GROUP))
    state_mask = state_row_group == state_col_group
    a_row = -jnp.exp(alog_row_ref[...])
    a_col = -jnp.exp(alog_col_ref[...])

    def split(a, n):
        parts = []
        for _ in range(n - 1):
            p = a.astype(_BF16)
            parts.append(p)
            a = a - p.astype(_F32)
        return parts + [a.astype(_BF16)]

    def cumsum_cols(a):
        return sum(jnp.dot(tri, p, preferred_element_type=_F32) for p in split(a, 3))

    def cumsum_rows(a):
        return sum(jnp.dot(p, trit, preferred_element_type=_F32) for p in split(a, 3))

    def spread(v, expand):
        return sum(jnp.dot(p, expand, preferred_element_type=_F32) for p in split(v, 2))

    def load_chunk(c):
        r0 = pl.multiple_of(c * L, L)
        xs = xc_ref[pl.ds(r0, L), 0:SSD_WIDTH]
        bc = xc_ref[pl.ds(r0, L), SSD_WIDTH:SSD_WIDTH + SSD_BC]
        cc = xc_ref[pl.ds(r0, L), SSD_WIDTH + SSD_BC:SSD_CONV_CH]
        dtc = _softplus(dt_ref[pl.ds(r0, L), :] + dtb_row_ref[...])
        a_c = dtc * a_row
        acs_c = cumsum_cols(a_c)
        return r0, xs, bc, cc, dtc, a_c, acs_c

    def state_step(state, bc, xs, weight_cols, total_row, expand):
        wexp = spread(weight_cols, expand)
        xw = (xs.astype(_F32) * wexp).astype(_BF16)
        new = lax.dot_general(bc, xw, _TN, preferred_element_type=_F32)
        decay = jnp.exp(spread(jnp.broadcast_to(total_row, (8, DT_COLS)), expand))[0:1, :]
        return state * decay + jnp.where(state_mask, new, 0.0)

    state_ref[...] = jnp.zeros_like(state_ref)

    def fwd_body(c, carry):
        r0, xs, bc, cc, dtc, a_c, acs_c = load_chunk(c)
        dtr = _softplus(dtt_ref[c] + dtb_col_ref[...])
        a_r = dtr * a_col
        acs_r = cumsum_rows(a_r)
        ecs_c = acs_c - a_c
        ecs_r = acs_r - a_r
        lane_bc = lax.broadcasted_iota(jnp.int32, (L, SSD_BC), 1)
        gmat = []
        for g in range(SSD_GROUPS):
            in_g = (lane_bc >= g * SSD_STATE) & (lane_bc < (g + 1) * SSD_STATE)
            cg = jnp.where(in_g, cc, jnp.zeros_like(cc))
            gmat.append(lax.dot_general(cg, bc, _NT, preferred_element_type=_F32))
        parts = []
        for h in range(SSD_HEADS):
            hb = SSD_HEADS + h
            lf = jnp.exp(jnp.where(causal, acs_c[:, h:h + 1] - acs_r[h:h + 1, :], neg_big))
            lb = jnp.exp(jnp.where(anti, ecs_r[hb:hb + 1, :] - ecs_c[:, hb:hb + 1], neg_big))
            w = gmat[h // SSD_HEADS_PER_GROUP] * (lf * dtr[h:h + 1, :] + lb * dtr[hb:hb + 1, :])
            parts.append(jnp.dot(w.astype(_BF16), xs[:, h * SSD_HEAD_DIM:(h + 1) * SSD_HEAD_DIM],
                                 preferred_element_type=_F32))
        y_diag = jnp.concatenate(parts, axis=-1)
        state = state_ref[...]
        y_off = (jnp.dot(cc, state.astype(_BF16), preferred_element_type=_F32)
                 * spread(jnp.exp(acs_c), expand_f))
        y_ref[pl.ds(r0, L), :] = y_diag + y_off
        total = acs_c[L - 1:L, :]
        state_ref[...] = state_step(state, bc, xs, jnp.exp(total - acs_c) * dtc, total, expand_f)
        return carry

    lax.fori_loop(0, n_chunks, fwd_body, 0, unroll=4)

    state_ref[...] = jnp.zeros_like(state_ref)

    def bwd_body(i, carry):
        c = n_chunks - 1 - i
        r0, xs, bc, cc, dtc, a_c, acs_c = load_chunk(c)
        ecs_c = acs_c - a_c
        total = acs_c[L - 1:L, :]
        state = state_ref[...]
        y_off = (jnp.dot(cc, state.astype(_BF16), preferred_element_type=_F32)
                 * spread(jnp.exp(total - ecs_c), expand_b))
        y = y_ref[pl.ds(r0, L), :] + y_off + dskip_ref[...] * xs.astype(_F32)
        gated = y * _silu(z_ref[pl.ds(r0, L), :].astype(_F32))
        o_ref[pl.ds(r0, L), :] = _rms_rows(gated, nw_ref[...]).astype(_BF16)
        state_ref[...] = state_step(state, bc, xs, jnp.exp(ecs_c) * dtc, total, expand_b)
        return carry

    lax.fori_loop(0, n_chunks, bwd_body, 0, unroll=4)


def _ssd_mixer(z, xbc, dt, dtt, conv_w, conv_b, dt_bias, a_log, d_skip, norm_w, batch, seq):
    n_chunks = seq // SSD_CHUNK
    per_b = lambda w: pl.BlockSpec((None, seq, w), lambda b: (b, 0, 0))
    dtb = dt_bias.reshape(DT_COLS).astype(_F32)
    alog = a_log.reshape(DT_COLS).astype(_F32)
    out = pl.pallas_call(
        _ssd_kernel,
        grid=(batch,),
        in_specs=[per_b(SSD_CONV_CH), per_b(SSD_WIDTH), per_b(DT_COLS),
                  pl.BlockSpec((n_chunks, DT_COLS, SSD_CHUNK), lambda b: (b, 0, 0)),
                  _full((SSD_CONV, SSD_CONV_CH)), _full((1, SSD_CONV_CH)),
                  _full((1, DT_COLS)), _full((DT_COLS, 1)), _full((1, DT_COLS)), _full((DT_COLS, 1)),
                  _full((1, SSD_WIDTH)), _full((1, SSD_WIDTH))],
        out_specs=per_b(SSD_WIDTH),
        out_shape=jax.ShapeDtypeStruct((batch, seq, SSD_WIDTH), _BF16),
        scratch_shapes=[pltpu.VMEM((seq + 16, SSD_CONV_CH), _F32), pltpu.VMEM((seq, SSD_CONV_CH), _BF16),
                        pltpu.VMEM((seq, SSD_WIDTH), _F32), pltpu.VMEM((SSD_BC, SSD_WIDTH), _F32)],
        compiler_params=_params("parallel"),
        name="ssd_mixer",
    )(xbc.reshape(batch, seq, SSD_CONV_CH), z.reshape(batch, seq, SSD_WIDTH), dt.reshape(batch, seq, DT_COLS),
      dtt, conv_w.astype(_F32), conv_b.reshape(1, SSD_CONV_CH).astype(_F32),
      dtb.reshape(1, DT_COLS), dtb.reshape(DT_COLS, 1), alog.reshape(1, DT_COLS), alog.reshape(DT_COLS, 1),
      jnp.repeat(d_skip.astype(_F32), SSD_HEAD_DIM).reshape(1, SSD_WIDTH), norm_w.reshape(1, SSD_WIDTH))
    return out.reshape(batch * seq, SSD_WIDTH)


def _proj_router_kernel(*refs, n_act):
    x_ref = refs[0]
    acts = refs[1:1 + n_act]
    ws = refs[1 + n_act:1 + 2 * n_act]
    nw_ref, wr_hi_ref, wr_lo_ref, br_ref, x1_ref, h_ref, route_ref, gate_ref, count_ref = refs[1 + 2 * n_act:]
    tm = x_ref.shape[0]
    x1 = x_ref[...]
    for a_ref, w_ref in zip(acts, ws):
        x1 = x1 + jnp.dot(a_ref[...], w_ref[...], preferred_element_type=_F32)
    x1_ref[...] = x1
    h = _rms_rows(x1, nw_ref[...])
    _store_row_tiles(h_ref, h)

    h_hi = h.astype(_BF16)
    h_lo = (h - h_hi.astype(_F32)).astype(_BF16)
    logits = (jnp.dot(h_hi, wr_hi_ref[...], preferred_element_type=_F32)
              + jnp.dot(h_lo, wr_hi_ref[...], preferred_element_type=_F32)
              + jnp.dot(h_hi, wr_lo_ref[...], preferred_element_type=_F32) + br_ref[...])
    lane = lax.broadcasted_iota(jnp.int32, (tm, V7X_LANES), 1)
    lane_f = lane.astype(_F32)
    ninf = jnp.float32(-jnp.inf)
    nolane = jnp.float32(V7X_LANES)

    def argmax_first(vals):
        m = jnp.max(vals, axis=-1, keepdims=True)
        idx = jnp.min(jnp.where(vals == m, lane_f, nolane), axis=-1, keepdims=True)
        return m, idx

    gl = jnp.where(lane < MOE_GROUPS, logits, ninf)
    gmax, gidx = argmax_first(gl)
    g_p = 1.0 / jnp.sum(jnp.exp(gl - gmax), axis=-1, keepdims=True)
    first = MOE_GROUPS + MOE_EXPERTS_PER_GROUP * gidx
    el = jnp.where((lane_f >= first) & (lane_f < first + MOE_EXPERTS_PER_GROUP), logits, ninf)
    e1, i1 = argmax_first(el)
    e2, i2 = argmax_first(jnp.where(lane_f == i1, ninf, el))
    r = jnp.exp(e2 - e1)
    p1 = 1.0 / (1.0 + r)
    gate_ref[...] = jnp.where(lane == 0, g_p * p1, jnp.where(lane == 1, g_p * (r * p1), 0.0))

    @pl.when(pl.program_id(0) == 0)
    def _():
        count_ref[...] = jnp.zeros_like(count_ref)

    chosen = (lane_f == i1) | (lane_f == i2)
    onehot = jnp.where(chosen, 1.0, 0.0)
    earlier = (lax.broadcasted_iota(jnp.int32, (tm, tm), 1)
               < lax.broadcasted_iota(jnp.int32, (tm, tm), 0)).astype(_BF16)
    prior = jnp.dot(earlier, onehot.astype(_BF16), preferred_element_type=_F32) + count_ref[0:1, :]
    rank1 = jnp.sum(jnp.where(lane_f == i1, prior, 0.0), axis=-1, keepdims=True)
    rank2 = jnp.sum(jnp.where(lane_f == i2, prior, 0.0), axis=-1, keepdims=True)
    count_ref[...] = count_ref[...] + jnp.sum(onehot, axis=0, keepdims=True)
    route = jnp.where(lane == 0, i1 - MOE_GROUPS, jnp.where(lane == 1, i2 - MOE_GROUPS,
                      jnp.where(lane == 2, rank1, jnp.where(lane == 3, rank2, 0.0))))
    route_ref[...] = route.astype(jnp.int32)


def _proj_router(x2d, acts, weights, norm_w, rg_w, rg_b, re_w, re_b):
    t = x2d.shape[0]
    tm = _tile(t, TOKEN_TILE)
    n_act = len(acts)
    n_log = MOE_GROUPS + MOE_EXPERTS
    wr = jnp.zeros((D_MODEL, V7X_LANES), _F32).at[:, :n_log].set(
        jnp.concatenate([rg_w.astype(_F32), re_w.reshape(D_MODEL, MOE_EXPERTS).astype(_F32)], axis=1))
    br = jnp.zeros((1, V7X_LANES), _F32).at[0, :n_log].set(
        jnp.concatenate([rg_b.astype(_F32), re_b.reshape(MOE_EXPERTS).astype(_F32)]))
    wr_hi = wr.astype(_BF16)
    wr_lo = (wr - wr_hi.astype(_F32)).astype(_BF16)
    row = lambda i: (i, 0)
    x1, h, route, gate, counts = pl.pallas_call(
        functools.partial(_proj_router_kernel, n_act=n_act),
        grid=(t // tm,),
        in_specs=[pl.BlockSpec((tm, D_MODEL), row)]
                 + [pl.BlockSpec((tm, a.shape[1]), row) for a in acts]
                 + [_full(w.shape) for w in weights]
                 + [_full((1, D_MODEL)), _full((D_MODEL, V7X_LANES)), _full((D_MODEL, V7X_LANES)),
                    _full((1, V7X_LANES))],
        out_specs=[pl.BlockSpec((tm, D_MODEL), row), pl.BlockSpec((tm * ROW_SUBLANES, V7X_LANES), row),
                   pl.BlockSpec((tm, V7X_LANES), row), pl.BlockSpec((tm, V7X_LANES), row),
                   _full((V7X_SUBLANES, V7X_LANES))],
        out_shape=[jax.ShapeDtypeStruct((t, D_MODEL), _F32),
                   jax.ShapeDtypeStruct((t * ROW_SUBLANES, V7X_LANES), _F32),
                   jax.ShapeDtypeStruct((t, V7X_LANES), jnp.int32), jax.ShapeDtypeStruct((t, V7X_LANES), _F32),
                   jax.ShapeDtypeStruct((V7X_SUBLANES, V7X_LANES), _F32)],
        compiler_params=_params("arbitrary"),
        name="proj_router",
    )(x2d, *acts, *[w.astype(_BF16) for w in weights], norm_w.reshape(1, D_MODEL), wr_hi, wr_lo, br)
    counts = counts[0, MOE_GROUPS:MOE_GROUPS + MOE_EXPERTS].astype(jnp.int32)
    return x1, h, route, gate, counts


def _store_row_tiles(ref, val):
    n = val.shape[0]
    for s in range(ROW_SUBLANES):
        ref[pl.ds(s, n, stride=ROW_SUBLANES), :] = val[:, s * V7X_LANES:(s + 1) * V7X_LANES]


def _load_row_tiles(ref, n):
    return jnp.concatenate([ref[pl.ds(s, n, stride=ROW_SUBLANES), :] for s in range(ROW_SUBLANES)], axis=1)


def _expert_kernel(bexp_ref, nused_ref, gidx_ref, gnext_ref, sidx_ref, h_hbm, w1_ref, w3_ref, w2_ref, out_hbm,
                   xbuf, ybuf, w13_bf, w2_bf, sem_in, sem_out):
    i = pl.program_id(0)

    @pl.when((i == 0) | (bexp_ref[i] != bexp_ref[jnp.maximum(i - 1, 0)]))
    def _():
        w13_bf[:, :MOE_HIDDEN] = w1_ref[...].astype(_BF16)
        w13_bf[:, MOE_HIDDEN:] = w3_ref[...].astype(_BF16)
        w2_bf[...] = w2_ref[...].astype(_BF16)

    n_used = nused_ref[0]
    slot = lax.rem(i, 2)
    other = 1 - slot
    tile = ROW_SUBLANES

    def gather_copy(buf_slot, r, tok):
        return pltpu.make_async_copy(h_hbm.at[pl.ds(pl.multiple_of(tok * tile, tile), tile)],
                                     xbuf.at[buf_slot, pl.ds(pl.multiple_of(r * tile, tile), tile)],
                                     sem_in.at[buf_slot])

    def scatter_copy(buf_slot, r, row):
        return pltpu.make_async_copy(ybuf.at[buf_slot, pl.ds(pl.multiple_of(r * tile, tile), tile)],
                                     out_hbm.at[pl.ds(pl.multiple_of(row * tile, tile), tile)],
                                     sem_out.at[buf_slot])

    def for_rows(fn):
        def body(j, carry):
            for u in range(DMA_UNROLL):
                fn(j * DMA_UNROLL + u)
            return carry
        lax.fori_loop(0, MOE_ROWS // DMA_UNROLL, body, 0)

    def start_gather(idx_ref, buf_slot):
        for_rows(lambda r: gather_copy(buf_slot, r, idx_ref[0, r]).start())

    def wait_gather(buf_slot):
        for r in range(MOE_ROWS):
            gather_copy(buf_slot, r, 0).wait()

    def start_scatter(buf_slot):
        for_rows(lambda r: scatter_copy(buf_slot, r, sidx_ref[0, r]).start(priority=1))

    def wait_scatter(buf_slot):
        for r in range(MOE_ROWS):
            scatter_copy(buf_slot, r, 0).wait()

    @pl.when(i == 0)
    def _():
        spare0 = out_hbm.shape[0] - 2 * MOE_ROWS * tile
        ybuf[...] = jnp.zeros_like(ybuf)
        fills = [pltpu.make_async_copy(ybuf.at[s], out_hbm.at[pl.ds(spare0 + s * MOE_ROWS * tile, MOE_ROWS * tile)],
                                       sem_out.at[s]) for s in range(2)]
        for f in fills:
            f.start()
        for f in fills:
            f.wait()
        start_gather(gidx_ref, 0)

    @pl.when(i + 1 < n_used)
    def _():
        start_gather(gnext_ref, other)

    @pl.when(i < n_used)
    def _():
        wait_gather(slot)
        xb = _load_row_tiles(xbuf.at[slot], MOE_ROWS).astype(_BF16)
        a = jnp.dot(xb, w13_bf[...], preferred_element_type=_F32)
        hid = _silu(a[:, :MOE_HIDDEN]) * a[:, MOE_HIDDEN:]
        y = jnp.dot(hid.astype(_BF16), w2_bf[...], preferred_element_type=_F32)

        @pl.when(i >= 2)
        def _():
            wait_scatter(slot)

        _store_row_tiles(ybuf.at[slot], y)
        start_scatter(slot)

        @pl.when(i == n_used - 1)
        def _():
            wait_scatter(slot)

            @pl.when(i >= 1)
            def _():
                wait_scatter(other)


def _expert_plan(route, counts, n_tok):
    n_assign = n_tok * MOE_TOP_K
    n_blocks = n_assign // MOE_ROWS + MOE_EXPERTS
    n_slots = n_blocks * MOE_ROWS
    experts = jnp.arange(MOE_EXPERTS, dtype=jnp.int32)
    nblk = (counts + MOE_ROWS - 1) // MOE_ROWS
    blk_end = jnp.cumsum(nblk)
    blk_first = blk_end - nblk
    n_used = blk_end[-1]
    bid = jnp.arange(n_blocks, dtype=jnp.int32)
    bexp = jnp.minimum(jnp.sum(bid[:, None] >= blk_end[None, :], axis=1), MOE_EXPERTS - 1).astype(jnp.int32)
    last_exp = jnp.max(jnp.where(counts > 0, experts, 0))
    bexp = jnp.where(bid < n_used, bexp, last_exp)
    e = route[:, 0:MOE_TOP_K]
    rank = route[:, MOE_TOP_K:2 * MOE_TOP_K]
    first_slot = jnp.sum(jnp.where(e[:, :, None] == experts, blk_first * MOE_ROWS, 0), axis=-1)
    dest = (first_slot + rank).reshape(n_assign)
    tok = jnp.broadcast_to(jnp.arange(n_tok, dtype=jnp.int32)[:, None], (n_tok, MOE_TOP_K))
    row = tok + jnp.arange(MOE_TOP_K, dtype=jnp.int32) * n_tok
    r = jnp.arange(MOE_ROWS, dtype=jnp.int32)
    spare = (MOE_TOP_K * n_tok + (bid[:, None] % 2) * MOE_ROWS + r[None, :]).reshape(n_slots)
    spare_at_dest = MOE_TOP_K * n_tok + ((dest // MOE_ROWS) % 2) * MOE_ROWS + dest % MOE_ROWS
    gidx = jnp.zeros((n_slots,), jnp.int32).at[dest].add(tok.reshape(n_assign))
    sidx = spare + jnp.zeros((n_slots,), jnp.int32).at[dest].add(row.reshape(n_assign) - spare_at_dest)
    shape3 = (n_blocks, 1, MOE_ROWS)
    return bexp, n_used.reshape(1).astype(jnp.int32), gidx.reshape(shape3), sidx.reshape(shape3), n_blocks


def _experts(h_tiles, route, counts, w1, w3, w2, layer):
    n_tok = h_tiles.shape[0] // ROW_SUBLANES
    bexp, n_used, gidx, sidx, n_blocks = _expert_plan(route, counts, n_tok)
    idx_block = (None, 1, MOE_ROWS)
    per_expert = lambda i, bexp, nu: (layer, bexp[i], 0, 0)
    grid_spec = pltpu.PrefetchScalarGridSpec(
        num_scalar_prefetch=2,
        grid=(n_blocks,),
        in_specs=[pl.BlockSpec(idx_block, lambda i, bexp, nu: (i, 0, 0), memory_space=pltpu.SMEM),
                  pl.BlockSpec(idx_block, lambda i, bexp, nu: (jnp.minimum(i + 1, n_blocks - 1), 0, 0),
                               memory_space=pltpu.SMEM),
                  pl.BlockSpec(idx_block, lambda i, bexp, nu: (i, 0, 0), memory_space=pltpu.SMEM),
                  pl.BlockSpec(memory_space=pl.ANY),
                  pl.BlockSpec((None, None, D_MODEL, MOE_HIDDEN), per_expert),
                  pl.BlockSpec((None, None, D_MODEL, MOE_HIDDEN), per_expert),
                  pl.BlockSpec((None, None, MOE_HIDDEN, D_MODEL), per_expert)],
        out_specs=pl.BlockSpec(memory_space=pl.ANY),
        scratch_shapes=[pltpu.VMEM((2, MOE_ROWS * ROW_SUBLANES, V7X_LANES), _F32),
                        pltpu.VMEM((2, MOE_ROWS * ROW_SUBLANES, V7X_LANES), _F32),
                        pltpu.VMEM((D_MODEL, 2 * MOE_HIDDEN), _BF16), pltpu.VMEM((MOE_HIDDEN, D_MODEL), _BF16),
                        pltpu.SemaphoreType.DMA((2,)), pltpu.SemaphoreType.DMA((2,))],
    )
    out_rows = MOE_TOP_K * n_tok + 2 * MOE_ROWS
    return pl.pallas_call(
        _expert_kernel,
        grid_spec=grid_spec,
        out_shape=jax.ShapeDtypeStruct((out_rows * ROW_SUBLANES, V7X_LANES), _F32),
        compiler_params=_params("arbitrary"),
        name="experts",
    )(bexp, n_used, gidx, gidx, sidx, h_tiles, w1, w3, w2)


def _combine(x1_ref, y0_ref, y1_ref, gate_ref):
    gate = gate_ref[...]
    tm = x1_ref.shape[0]
    return x1_ref[...] + gate[:, 0:1] * _load_row_tiles(y0_ref, tm) + gate[:, 1:2] * _load_row_tiles(y1_ref, tm)


def _combine_kernel(x1_ref, y0_ref, y1_ref, gate_ref, o_ref):
    o_ref[...] = _combine(x1_ref, y0_ref, y1_ref, gate_ref)


def _combine_dft_kernel(x1_ref, y0_ref, y1_ref, gate_ref, nw_ref, cosc_ref, sinc_ref, x_ref, hc_ref, hs_ref):
    x = _combine(x1_ref, y0_ref, y1_ref, gate_ref)
    x_ref[...] = x
    h = _rms_rows(x, nw_ref[...]).astype(_BF16)
    for g in range(FOURIER_GROUPS):
        c0 = g * FOURIER_GROUP_DIM
        hg = h[:, c0:c0 + FOURIER_GROUP_DIM]
        hc_ref[:, c0:c0 + FOURIER_GROUP_DIM] = jnp.dot(hg, cosc_ref[...], preferred_element_type=_F32).astype(_BF16)
        hs_ref[:, c0:c0 + FOURIER_GROUP_DIM] = jnp.dot(hg, sinc_ref[...], preferred_element_type=_F32).astype(_BF16)


def _moe_combine(x1, y, gate, dft=None):
    t = x1.shape[0]
    tm = _tile(t, TOKEN_TILE)
    row = lambda i: (i, 0)
    second = lambda i: (i + t // tm, 0)
    y_block = (tm * ROW_SUBLANES, V7X_LANES)
    in_specs = [pl.BlockSpec((tm, D_MODEL), row), pl.BlockSpec(y_block, row),
                pl.BlockSpec(y_block, second), pl.BlockSpec((tm, V7X_LANES), row)]
    if dft is None:
        return pl.pallas_call(
            _combine_kernel, grid=(t // tm,), in_specs=in_specs,
            out_specs=pl.BlockSpec((tm, D_MODEL), row),
            out_shape=jax.ShapeDtypeStruct((t, D_MODEL), _F32),
            compiler_params=_params("parallel"), name="moe_combine",
        )(x1, y, y, gate)
    norm_w, cosc, sinc = dft
    return pl.pallas_call(
        _combine_dft_kernel, grid=(t // tm,),
        in_specs=in_specs + [_full((1, D_MODEL)), _full(cosc.shape), _full(sinc.shape)],
        out_specs=[pl.BlockSpec((tm, D_MODEL), row)] * 3,
        out_shape=[jax.ShapeDtypeStruct((t, D_MODEL), _F32), jax.ShapeDtypeStruct((t, D_MODEL), _BF16),
                   jax.ShapeDtypeStruct((t, D_MODEL), _BF16)],
        compiler_params=_params("parallel"), name="moe_combine_dft",
    )(x1, y, y, gate, norm_w.reshape(1, D_MODEL), cosc, sinc)


def _seq_dft_kernel(ca_ref, sa_ref, cb_ref, sb_ref, hc_ref, hs_ref, o_ref, cos_tab, nsin_tab):
    tq = cos_tab.shape[0]

    @pl.when(pl.program_id(1) == 0)
    def _():
        cb = cb_ref[...]
        sb = sb_ref[...]
        for g in range(tq // DFT_FINE):
            j1 = pl.program_id(0) * (tq // DFT_FINE) + g
            ca = ca_ref[pl.ds(j1, 1), :]
            sa = sa_ref[pl.ds(j1, 1), :]
            cos_tab[g * DFT_FINE:(g + 1) * DFT_FINE, :] = (ca * cb - sa * sb).astype(_BF16)
            nsin_tab[g * DFT_FINE:(g + 1) * DFT_FINE, :] = (-(sa * cb + ca * sb)).astype(_BF16)

    o_ref[...] = (jnp.dot(cos_tab[...], hc_ref[...], preferred_element_type=_F32)
                  + jnp.dot(nsin_tab[...], hs_ref[...], preferred_element_type=_F32)).astype(_BF16)


def _seq_dft(hc, hs, batch, seq):
    tq = _tile(seq, DFT_ROW_TILE)
    assert tq % DFT_FINE == 0
    n_coarse = seq // DFT_FINE
    ca, sa = _dft_tables(jnp.arange(n_coarse, dtype=jnp.int32) * DFT_FINE, seq, seq ** -0.5)
    cb, sb = _dft_tables(jnp.arange(DFT_FINE, dtype=jnp.int32), seq, 1.0)
    per_b = pl.BlockSpec((None, seq, D_MODEL), lambda i, b: (b, 0, 0))
    out = pl.pallas_call(
        _seq_dft_kernel, grid=(seq // tq, batch),
        in_specs=[_full((n_coarse, seq)), _full((n_coarse, seq)), _full((DFT_FINE, seq)), _full((DFT_FINE, seq)),
                  per_b, per_b],
        out_specs=pl.BlockSpec((None, tq, D_MODEL), lambda i, b: (b, i, 0)),
        out_shape=jax.ShapeDtypeStruct((batch, seq, D_MODEL), _BF16),
        scratch_shapes=[pltpu.VMEM((tq, seq), _BF16), pltpu.VMEM((tq, seq), _BF16)],
        compiler_params=_params("parallel", "arbitrary"), name="seq_dft",
    )(ca, sa, cb, sb, hc.reshape(batch, seq, D_MODEL), hs.reshape(batch, seq, D_MODEL))
    return out.reshape(batch * seq, D_MODEL)


def _dft_tables(rows, n, scale):
    k = jnp.arange(n, dtype=jnp.int32)
    m = (rows[:, None] * k[None, :]) % n
    ang = m.astype(_F32) * (2.0 * math.pi / n)
    return jnp.cos(ang) * scale, jnp.sin(ang) * scale


def kernel(x, norm_mix, norm_ffn, w_in, q_norm, k_norm, lambda_q1, lambda_k1, lambda_q2, lambda_k2,
           attn_subln, conv_w, conv_b, dt_bias, a_log, d_skip, ssd_norm, w_out, w_fourier,
           router_group_w, router_group_b, router_expert_w, router_expert_b,
           expert_w1, expert_w3, expert_w2):
    batch, seq, _ = x.shape
    depth = norm_mix.shape[0]
    xt = x.reshape(batch * seq, D_MODEL)
    pending = None
    for l in range(depth):
        i = l // 2
        if l % 2 == 0:
            if pending is not None:
                xt = _moe_combine(*pending)
            lambda_init = 0.8 - 0.6 * math.exp(-0.3 * l)
            q, k, v, z, xbc, dt, dtt = _in_proj(xt, norm_mix[l], w_in[i], q_norm[i], k_norm[i])
            attn = _diff_attention(q, k, v, lambda_q1[i], lambda_k1[i], lambda_q2[i], lambda_k2[i],
                                   attn_subln[i], lambda_init, batch, seq)
            ssd = _ssd_mixer(z, xbc, dt, dtt, conv_w[i], conv_b[i], dt_bias[i], a_log[i], d_skip[i],
                             ssd_norm[i], batch, seq)
            acts = [attn, ssd]
            weights = [w_out[i][:ATTN_WIDTH], w_out[i][ATTN_WIDTH:]]
        else:
            cos_c, sin_c = _dft_tables(jnp.arange(FOURIER_GROUP_DIM, dtype=jnp.int32), FOURIER_GROUP_DIM,
                                       FOURIER_GROUP_DIM ** -0.5)
            dft = (norm_mix[l], cos_c.astype(_BF16), sin_c.astype(_BF16))
            if pending is None:
                zeros = jnp.zeros((2 * batch * seq * ROW_SUBLANES, V7X_LANES), _F32)
                pending = (xt, zeros, jnp.zeros((batch * seq, V7X_LANES), _F32))
            xt, hc, hs = _moe_combine(*pending, dft=dft)
            f = _seq_dft(hc, hs, batch, seq)
            acts = [f]
            weights = [w_fourier[i]]
        x1, h, route, gate, counts = _proj_router(xt, acts, weights, norm_ffn[l], router_group_w[l],
                                                  router_group_b[l], router_expert_w[l], router_expert_b[l])
        y = _experts(h, route, counts, expert_w1, expert_w3, expert_w2, l)
        pending = (x1, y, gate)
    xt = _moe_combine(*pending)
    return xt.reshape(batch, seq, D_MODEL)
```

```python
import functools
import math

import jax
import jax.numpy as jnp
from jax import lax
from jax.experimental import pallas as pl
from jax.experimental.pallas import tpu as pltpu

D_MODEL = 1024
N_DIFF_HEADS = 4
DIFF_QK_DIM = 64
DIFF_V_DIM = 2 * DIFF_QK_DIM
ATTN_WIDTH = N_DIFF_HEADS * DIFF_V_DIM
SSD_HEAD_DIM = 64
SSD_WIDTH = D_MODEL // 2
SSD_HEADS = SSD_WIDTH // SSD_HEAD_DIM
SSD_GROUPS = 2
SSD_HEADS_PER_GROUP = SSD_HEADS // SSD_GROUPS
SSD_STATE = 64
SSD_CONV = 5
SSD_CHUNK = 128
SSD_BC = SSD_GROUPS * SSD_STATE
SSD_CONV_CH = SSD_WIDTH + 2 * SSD_BC
Q_COLS = N_DIFF_HEADS * 2 * DIFF_QK_DIM
DT_COLS = 2 * SSD_HEADS
FOURIER_GROUPS = 4
FOURIER_GROUP_DIM = D_MODEL // FOURIER_GROUPS
MOE_GROUPS = 4
MOE_EXPERTS_PER_GROUP = 8
MOE_EXPERTS = MOE_GROUPS * MOE_EXPERTS_PER_GROUP
MOE_TOP_K = 2
MOE_HIDDEN = 256
RMS_EPS = 1e-6
LOG2_E = math.log2(math.e)

V7X_LANES = 128
V7X_VMEM_LIMIT = 48 * 1024 * 1024
TOKEN_TILE = 512
ATTN_Q_TILE = 256
DFT_ROW_TILE = 512
MOE_ROWS = 256
V7X_SUBLANES = 8
ROW_SUBLANES = D_MODEL // V7X_LANES
DFT_FINE = 64
DFT_REV_BLOCK = 128
DMA_UNROLL = 16
CONV_ROWS = 128

_F32 = jnp.float32
_BF16 = jnp.bfloat16
_NT = (((1,), (1,)), ((), ()))
_TN = (((0,), (0,)), ((), ()))


def _params(*sem):
    return pltpu.CompilerParams(dimension_semantics=sem, vmem_limit_bytes=V7X_VMEM_LIMIT)


def _tile(n, want):
    t = min(n, want)
    assert n % t == 0, (n, t)
    return t


def _full(shape):
    return pl.BlockSpec(shape, lambda *_: (0,) * len(shape))


def _rms_rows(x, w):
    ms = jnp.mean(x * x, axis=-1, keepdims=True)
    return x * lax.rsqrt(ms + RMS_EPS) * w


def _silu(x):
    return x * jax.nn.sigmoid(x)


def _softplus(x):
    return jnp.maximum(x, 0.0) + jnp.log1p(jnp.exp(-jnp.abs(x)))


def _in_proj_kernel(x_ref, nw_ref, wqkv_ref, wz_ref, wxbc_ref, wdt_ref, wdtt_ref, qn_ref, kn_ref,
                    q_ref, k_ref, v_ref, z_ref, xbc_ref, dt_ref, dtt_ref):
    tm = x_ref.shape[0]
    h = _rms_rows(x_ref[...], nw_ref[...]).astype(_BF16)
    qkv = jnp.dot(h, wqkv_ref[...], preferred_element_type=_F32)
    low = lax.broadcasted_iota(jnp.int32, (tm, V7X_LANES), 1) < DIFF_QK_DIM

    def qk_norm(slab, w):
        sq = slab * slab
        s_lo = jnp.sum(jnp.where(low, sq, 0.0), axis=-1, keepdims=True)
        s_hi = jnp.sum(jnp.where(low, 0.0, sq), axis=-1, keepdims=True)
        ms = jnp.where(low, s_lo, s_hi) * (1.0 / DIFF_QK_DIM)
        return slab * lax.rsqrt(ms + RMS_EPS) * w

    scale = DIFF_QK_DIM ** -0.5 * LOG2_E
    for hd in range(N_DIFF_HEADS):
        c0 = hd * DIFF_V_DIM
        q_ref[:, c0:c0 + DIFF_V_DIM] = (qk_norm(qkv[:, c0:c0 + DIFF_V_DIM], qn_ref[...]) * scale).astype(_BF16)
        k_ref[:, c0:c0 + DIFF_V_DIM] = qk_norm(qkv[:, Q_COLS + c0:Q_COLS + c0 + DIFF_V_DIM], kn_ref[...]).astype(_BF16)
    v_ref[...] = qkv[:, 2 * Q_COLS:].astype(_BF16)
    z_ref[...] = jnp.dot(h, wz_ref[...], preferred_element_type=_F32).astype(_BF16)
    xbc_ref[...] = jnp.dot(h, wxbc_ref[...], preferred_element_type=_F32)
    dt_ref[...] = jnp.dot(h, wdt_ref[...], preferred_element_type=_F32)
    dtt = lax.dot_general(wdtt_ref[...], h, _NT, preferred_element_type=_F32)
    for j in range(tm // SSD_CHUNK):
        dtt_ref[j] = dtt[:, j * SSD_CHUNK:(j + 1) * SSD_CHUNK]


def _in_proj(x2d, norm_w, w_in, q_norm, k_norm):
    t = x2d.shape[0]
    tm = _tile(t, TOKEN_TILE)
    c_qkv = 3 * Q_COLS
    c_z = c_qkv + SSD_WIDTH
    c_xbc = c_z + SSD_CONV_CH
    wb = w_in.astype(_BF16)
    wqkv, wz, wxbc, wdt = wb[:, :c_qkv], wb[:, c_qkv:c_z], wb[:, c_z:c_xbc], wb[:, c_xbc:]
    row = lambda i: (i, 0)
    outs = pl.pallas_call(
        _in_proj_kernel,
        grid=(t // tm,),
        in_specs=[pl.BlockSpec((tm, D_MODEL), row), _full((1, D_MODEL)),
                  _full((D_MODEL, c_qkv)), _full((D_MODEL, SSD_WIDTH)), _full((D_MODEL, SSD_CONV_CH)),
                  _full((D_MODEL, DT_COLS)), _full((DT_COLS, D_MODEL)),
                  _full((1, DIFF_V_DIM)), _full((1, DIFF_V_DIM))],
        out_specs=[pl.BlockSpec((tm, Q_COLS), row), pl.BlockSpec((tm, Q_COLS), row),
                   pl.BlockSpec((tm, ATTN_WIDTH), row), pl.BlockSpec((tm, SSD_WIDTH), row),
                   pl.BlockSpec((tm, SSD_CONV_CH), row), pl.BlockSpec((tm, DT_COLS), row),
                   pl.BlockSpec((tm // SSD_CHUNK, DT_COLS, SSD_CHUNK), lambda i: (i, 0, 0))],
        out_shape=[jax.ShapeDtypeStruct((t, Q_COLS), _BF16), jax.ShapeDtypeStruct((t, Q_COLS), _BF16),
                   jax.ShapeDtypeStruct((t, ATTN_WIDTH), _BF16), jax.ShapeDtypeStruct((t, SSD_WIDTH), _BF16),
                   jax.ShapeDtypeStruct((t, SSD_CONV_CH), _F32), jax.ShapeDtypeStruct((t, DT_COLS), _F32),
                   jax.ShapeDtypeStruct((t // SSD_CHUNK, DT_COLS, SSD_CHUNK), _F32)],
        compiler_params=_params("parallel"),
        name="in_proj",
    )(x2d, norm_w.reshape(1, D_MODEL), wqkv, wz, wxbc, wdt, wdt.T,
      jnp.tile(q_norm, 2).reshape(1, DIFF_V_DIM), jnp.tile(k_norm, 2).reshape(1, DIFF_V_DIM))
    return outs


def _attn_kernel(slope_ref, lq1_ref, lk1_ref, lq2_ref, lk2_ref, q_ref, k_ref, v_ref, subln_ref, o_ref,
                 bias_ref, *, lambda_init):
    tq = q_ref.shape[0]
    seq = k_ref.shape[0]
    hd = pl.program_id(0)
    qi = pl.program_id(1)

    @pl.when(pl.program_id(2) == 0)
    def _():
        row = qi * tq + lax.broadcasted_iota(jnp.int32, (tq, seq), 0)
        col = lax.broadcasted_iota(jnp.int32, (tq, seq), 1)
        bias_ref[...] = (slope_ref[hd] * LOG2_E) * jnp.abs(row - col).astype(_F32)

    lam = (jnp.exp(jnp.sum(lq1_ref[...] * lk1_ref[...], keepdims=True))
           - jnp.exp(jnp.sum(lq2_ref[...] * lk2_ref[...], keepdims=True)) + lambda_init)
    q = q_ref[...]
    k = k_ref[...]
    low = lax.broadcasted_iota(jnp.int32, (tq, V7X_LANES), 1) < DIFF_QK_DIM
    zero = jnp.zeros_like(q)

    def branch(qc):
        s = lax.dot_general(qc, k, _NT, preferred_element_type=_F32) - bias_ref[...]
        p = jnp.exp2(s - jnp.max(s, axis=-1, keepdims=True))
        return p, jnp.sum(p, axis=-1, keepdims=True)

    p1, l1 = branch(jnp.where(low, q, zero))
    p2, l2 = branch(jnp.where(low, zero, q))
    w = p1 * (1.0 / l1) - p2 * (lam / l2)
    o = jnp.dot(w.astype(_BF16), v_ref[...], preferred_element_type=_F32)
    o_ref[...] = (_rms_rows(o, subln_ref[...]) * (1.0 - lambda_init)).astype(_BF16)


def _diff_attention(q, k, v, lq1, lk1, lq2, lk2, subln, lambda_init, batch, seq):
    tq = _tile(seq, ATTN_Q_TILE)
    slopes = jnp.asarray([2.0 ** (-8.0 * (h + 1) / N_DIFF_HEADS) for h in range(N_DIFF_HEADS)], _F32)
    vec = lambda a: a.reshape(1, -1).astype(_F32)
    q3, k3, v3 = (a.reshape(batch, seq, ATTN_WIDTH) for a in (q, k, v))
    kv_spec = pl.BlockSpec((None, seq, DIFF_V_DIM), lambda h, i, b: (b, 0, h))
    q_spec = pl.BlockSpec((None, tq, DIFF_V_DIM), lambda h, i, b: (b, i, h))
    out = pl.pallas_call(
        functools.partial(_attn_kernel, lambda_init=lambda_init),
        grid=(N_DIFF_HEADS, seq // tq, batch),
        in_specs=[pl.BlockSpec(memory_space=pltpu.SMEM)] + [_full((1, DIFF_QK_DIM))] * 4
                 + [q_spec, kv_spec, kv_spec, _full((1, DIFF_V_DIM))],
        out_specs=q_spec,
        out_shape=jax.ShapeDtypeStruct((batch, seq, ATTN_WIDTH), _BF16),
        scratch_shapes=[pltpu.VMEM((tq, seq), _F32)],
        compiler_params=_params("parallel", "parallel", "arbitrary"),
        name="diff_attn",
    )(slopes, vec(lq1), vec(lk1), vec(lq2), vec(lk2), q3, k3, v3, vec(subln))
    return out.reshape(batch * seq, ATTN_WIDTH)


def _ssd_kernel(xbc_ref, z_ref, dt_ref, dtt_ref, cw_ref, cb_ref, dtb_row_ref, dtb_col_ref,
                alog_row_ref, alog_col_ref, dskip_ref, nw_ref, o_ref,
                xpad_ref, xc_ref, y_ref, state_ref):
    seq = xbc_ref.shape[0]
    n_chunks = seq // SSD_CHUNK
    L = SSD_CHUNK
    halo = 8
    pad = SSD_CONV // 2

    xpad_ref[0:halo, :] = jnp.zeros((halo, SSD_CONV_CH), _F32)
    xpad_ref[halo + seq:2 * halo + seq, :] = jnp.zeros((halo, SSD_CONV_CH), _F32)
    xpad_ref[halo:halo + seq, :] = xbc_ref[...]
    for rb in range(seq // CONV_ROWS):
        base = rb * CONV_ROWS + halo - pad
        acc = jnp.broadcast_to(cb_ref[...], (CONV_ROWS, SSD_CONV_CH))
        for j in range(SSD_CONV):
            acc = acc + cw_ref[j:j + 1, :] * xpad_ref[base + j:base + j + CONV_ROWS, :]
        xc_ref[rb * CONV_ROWS:(rb + 1) * CONV_ROWS, :] = _silu(acc).astype(_BF16)

    sub = lax.broadcasted_iota(jnp.int32, (L, L), 0)
    lane = lax.broadcasted_iota(jnp.int32, (L, L), 1)
    causal = lane <= sub
    anti = lane >= sub
    tri = causal.astype(_BF16)
    trit = anti.astype(_BF16)
    neg_big = jnp.float32(-1e30)
    head_of_col = lax.broadcasted_iota(jnp.int32, (DT_COLS, SSD_WIDTH), 1) // SSD_HEAD_DIM
    dtcol = lax.broadcasted_iota(jnp.int32, (DT_COLS, SSD_WIDTH), 0)
    expand_f = (dtcol == head_of_col).astype(_BF16)
    expand_b = (dtcol == head_of_col + SSD_HEADS).astype(_BF16)
    state_row_group = lax.broadcasted_iota(jnp.int32, (SSD_BC, SSD_WIDTH), 0) // SSD_STATE
    state_col_group = (lax.broadcasted_iota(jnp.int32, (SSD_BC, SSD_WIDTH), 1)
                       // (SSD_HEAD_DIM * SSD_HEADS_PER_GROUP))
    state_mask = state_row_group == state_col_group
    a_row = -jnp.exp(alog_row_ref[...])
    a_col = -jnp.exp(alog_col_ref[...])

    def split(a, n):
        parts = []
        for _ in range(n - 1):
            p = a.astype(_BF16)
            parts.append(p)
            a = a - p.astype(_F32)
        return parts + [a.astype(_BF16)]

    def cumsum_cols(a):
        return sum(jnp.dot(tri, p, preferred_element_type=_F32) for p in split(a, 3))

    def cumsum_rows(a):
        return sum(jnp.dot(p, trit, preferred_element_type=_F32) for p in split(a, 3))

    def spread(v, expand):
        return sum(jnp.dot(p, expand, preferred_element_type=_F32) for p in split(v, 2))

    def load_chunk(c):
        r0 = pl.multiple_of(c * L, L)
        xs = xc_ref[pl.ds(r0, L), 0:SSD_WIDTH]
        bc = xc_ref[pl.ds(r0, L), SSD_WIDTH:SSD_WIDTH + SSD_BC]
        cc = xc_ref[pl.ds(r0, L), SSD_WIDTH + SSD_BC:SSD_CONV_CH]
        dtc = _softplus(dt_ref[pl.ds(r0, L), :] + dtb_row_ref[...])
        a_c = dtc * a_row
        acs_c = cumsum_cols(a_c)
        return r0, xs, bc, cc, dtc, a_c, acs_c

    def state_step(state, bc, xs, weight_cols, total_row, expand):
        wexp = spread(weight_cols, expand)
        xw = (xs.astype(_F32) * wexp).astype(_BF16)
        new = lax.dot_general(bc, xw, _TN, preferred_element_type=_F32)
        decay = jnp.exp(spread(jnp.broadcast_to(total_row, (8, DT_COLS)), expand))[0:1, :]
        return state * decay + jnp.where(state_mask, new, 0.0)

    state_ref[...] = jnp.zeros_like(state_ref)

    def fwd_body(c, carry):
        r0, xs, bc, cc, dtc, a_c, acs_c = load_chunk(c)
        dtr = _softplus(dtt_ref[c] + dtb_col_ref[...])
        a_r = dtr * a_col
        acs_r = cumsum_rows(a_r)
        ecs_c = acs_c - a_c
        ecs_r = acs_r - a_r
        lane_bc = lax.broadcasted_iota(jnp.int32, (L, SSD_BC), 1)
        gmat = []
        for g in range(SSD_GROUPS):
            in_g = (lane_bc >= g * SSD_STATE) & (lane_bc < (g + 1) * SSD_STATE)
            cg = jnp.where(in_g, cc, jnp.zeros_like(cc))
            gmat.append(lax.dot_general(cg, bc, _NT, preferred_element_type=_F32))
        parts = []
        for h in range(SSD_HEADS):
            hb = SSD_HEADS + h
            lf = jnp.exp(jnp.where(causal, acs_c[:, h:h + 1] - acs_r[h:h + 1, :], neg_big))
            lb = jnp.exp(jnp.where(anti, ecs_r[hb:hb + 1, :] - ecs_c[:, hb:hb + 1], neg_big))
            w = gmat[h // SSD_HEADS_PER_GROUP] * (lf * dtr[h:h + 1, :] + lb * dtr[hb:hb + 1, :])
            parts.append(jnp.dot(w.astype(_BF16), xs[:, h * SSD_HEAD_DIM:(h + 1) * SSD_HEAD_DIM],
                                 preferred_element_type=_F32))
        y_diag = jnp.concatenate(parts, axis=-1)
        state = state_ref[...]
        y_off = (jnp.dot(cc, state.astype(_BF16), preferred_element_type=_F32)
                 * spread(jnp.exp(acs_c), expand_f))
        y_ref[pl.ds(r0, L), :] = y_diag + y_off
        total = acs_c[L - 1:L, :]
        state_ref[...] = state_step(state, bc, xs, jnp.exp(total - acs_c) * dtc, total, expand_f)
        return carry

    lax.fori_loop(0, n_chunks, fwd_body, 0, unroll=4)

    state_ref[...] = jnp.zeros_like(state_ref)

    def bwd_body(i, carry):
        c = n_chunks - 1 - i
        r0, xs, bc, cc, dtc, a_c, acs_c = load_chunk(c)
        ecs_c = acs_c - a_c
        total = acs_c[L - 1:L, :]
        state = state_ref[...]
        y_off = (jnp.dot(cc, state.astype(_BF16), preferred_element_type=_F32)
                 * spread(jnp.exp(total - ecs_c), expand_b))
        y = y_ref[pl.ds(r0, L), :] + y_off + dskip_ref[...] * xs.astype(_F32)
        gated = y * _silu(z_ref[pl.ds(r0, L), :].astype(_F32))
        o_ref[pl.ds(r0, L), :] = _rms_rows(gated, nw_ref[...]).astype(_BF16)
        state_ref[...] = state_step(state, bc, xs, jnp.exp(ecs_c) * dtc, total, expand_b)
        return carry

    lax.fori_loop(0, n_chunks, bwd_body, 0, unroll=4)


def _ssd_mixer(z, xbc, dt, dtt, conv_w, conv_b, dt_bias, a_log, d_skip, norm_w, batch, seq):
    n_chunks = seq // SSD_CHUNK
    per_b = lambda w: pl.BlockSpec((None, seq, w), lambda b: (b, 0, 0))
    dtb = dt_bias.reshape(DT_COLS).astype(_F32)
    alog = a_log.reshape(DT_COLS).astype(_F32)
    out = pl.pallas_call(
        _ssd_kernel,
        grid=(batch,),
        in_specs=[per_b(SSD_CONV_CH), per_b(SSD_WIDTH), per_b(DT_COLS),
                  pl.BlockSpec((n_chunks, DT_COLS, SSD_CHUNK), lambda b: (b, 0, 0)),
                  _full((SSD_CONV, SSD_CONV_CH)), _full((1, SSD_CONV_CH)),
                  _full((1, DT_COLS)), _full((DT_COLS, 1)), _full((1, DT_COLS)), _full((DT_COLS, 1)),
                  _full((1, SSD_WIDTH)), _full((1, SSD_WIDTH))],
        out_specs=per_b(SSD_WIDTH),
        out_shape=jax.ShapeDtypeStruct((batch, seq, SSD_WIDTH), _BF16),
        scratch_shapes=[pltpu.VMEM((seq + 16, SSD_CONV_CH), _F32), pltpu.VMEM((seq, SSD_CONV_CH), _BF16),
                        pltpu.VMEM((seq, SSD_WIDTH), _F32), pltpu.VMEM((SSD_BC, SSD_WIDTH), _F32)],
        compiler_params=_params("parallel"),
        name="ssd_mixer",
    )(xbc.reshape(batch, seq, SSD_CONV_CH), z.reshape(batch, seq, SSD_WIDTH), dt.reshape(batch, seq, DT_COLS),
      dtt, conv_w.astype(_F32), conv_b.reshape(1, SSD_CONV_CH).astype(_F32),
      dtb.reshape(1, DT_COLS), dtb.reshape(DT_COLS, 1), alog.reshape(1, DT_COLS), alog.reshape(DT_COLS, 1),
      jnp.repeat(d_skip.astype(_F32), SSD_HEAD_DIM).reshape(1, SSD_WIDTH), norm_w.reshape(1, SSD_WIDTH))
    return out.reshape(batch * seq, SSD_WIDTH)


def _proj_router_kernel(*refs, n_act):
    x_ref = refs[0]
    acts = refs[1:1 + n_act]
    ws = refs[1 + n_act:1 + 2 * n_act]
    nw_ref, wr_hi_ref, wr_lo_ref, br_ref, x1_ref, h_ref, route_ref, gate_ref, count_ref = refs[1 + 2 * n_act:]
    tm = x_ref.shape[0]
    x1 = x_ref[...]
    for a_ref, w_ref in zip(acts, ws):
        x1 = x1 + jnp.dot(a_ref[...], w_ref[...], preferred_element_type=_F32)
    x1_ref[...] = x1
    h = _rms_rows(x1, nw_ref[...])
    _store_row_tiles(h_ref, h)

    h_hi = h.astype(_BF16)
    h_lo = (h - h_hi.astype(_F32)).astype(_BF16)
    logits = (jnp.dot(h_hi, wr_hi_ref[...], preferred_element_type=_F32)
              + jnp.dot(h_lo, wr_hi_ref[...], preferred_element_type=_F32)
              + jnp.dot(h_hi, wr_lo_ref[...], preferred_element_type=_F32) + br_ref[...])
    lane = lax.broadcasted_iota(jnp.int32, (tm, V7X_LANES), 1)
    lane_f = lane.astype(_F32)
    ninf = jnp.float32(-jnp.inf)
    nolane = jnp.float32(V7X_LANES)

    def argmax_first(vals):
        m = jnp.max(vals, axis=-1, keepdims=True)
        idx = jnp.min(jnp.where(vals == m, lane_f, nolane), axis=-1, keepdims=True)
        return m, idx

    gl = jnp.where(lane < MOE_GROUPS, logits, ninf)
    gmax, gidx = argmax_first(gl)
    g_p = 1.0 / jnp.sum(jnp.exp(gl - gmax), axis=-1, keepdims=True)
    first = MOE_GROUPS + MOE_EXPERTS_PER_GROUP * gidx
    el = jnp.where((lane_f >= first) & (lane_f < first + MOE_EXPERTS_PER_GROUP), logits, ninf)
    e1, i1 = argmax_first(el)
    e2, i2 = argmax_first(jnp.where(lane_f == i1, ninf, el))
    r = jnp.exp(e2 - e1)
    p1 = 1.0 / (1.0 + r)
    gate_ref[...] = jnp.where(lane == 0, g_p * p1, jnp.where(lane == 1, g_p * (r * p1), 0.0))

    @pl.when(pl.program_id(0) == 0)
    def _():
        count_ref[...] = jnp.zeros_like(count_ref)

    chosen = (lane_f == i1) | (lane_f == i2)
    onehot = jnp.where(chosen, 1.0, 0.0)
    earlier = (lax.broadcasted_iota(jnp.int32, (tm, tm), 1)
               < lax.broadcasted_iota(jnp.int32, (tm, tm), 0)).astype(_BF16)
    prior = jnp.dot(earlier, onehot.astype(_BF16), preferred_element_type=_F32) + count_ref[0:1, :]
    rank1 = jnp.sum(jnp.where(lane_f == i1, prior, 0.0), axis=-1, keepdims=True)
    rank2 = jnp.sum(jnp.where(lane_f == i2, prior, 0.0), axis=-1, keepdims=True)
    count_ref[...] = count_ref[...] + jnp.sum(onehot, axis=0, keepdims=True)
    route = jnp.where(lane == 0, i1 - MOE_GROUPS, jnp.where(lane == 1, i2 - MOE_GROUPS,
                      jnp.where(lane == 2, rank1, jnp.where(lane == 3, rank2, 0.0))))
    route_ref[...] = route.astype(jnp.int32)


def _proj_router(x2d, acts, weights, norm_w, rg_w, rg_b, re_w, re_b):
    t = x2d.shape[0]
    tm = _tile(t, TOKEN_TILE)
    n_act = len(acts)
    n_log = MOE_GROUPS + MOE_EXPERTS
    wr = jnp.zeros((D_MODEL, V7X_LANES), _F32).at[:, :n_log].set(
        jnp.concatenate([rg_w.astype(_F32), re_w.reshape(D_MODEL, MOE_EXPERTS).astype(_F32)], axis=1))
    br = jnp.zeros((1, V7X_LANES), _F32).at[0, :n_log].set(
        jnp.concatenate([rg_b.astype(_F32), re_b.reshape(MOE_EXPERTS).astype(_F32)]))
    wr_hi = wr.astype(_BF16)
    wr_lo = (wr - wr_hi.astype(_F32)).astype(_BF16)
    row = lambda i: (i, 0)
    x1, h, route, gate, counts = pl.pallas_call(
        functools.partial(_proj_router_kernel, n_act=n_act),
        grid=(t // tm,),
        in_specs=[pl.BlockSpec((tm, D_MODEL), row)]
                 + [pl.BlockSpec((tm, a.shape[1]), row) for a in acts]
                 + [_full(w.shape) for w in weights]
                 + [_full((1, D_MODEL)), _full((D_MODEL, V7X_LANES)), _full((D_MODEL, V7X_LANES)),
                    _full((1, V7X_LANES))],
        out_specs=[pl.BlockSpec((tm, D_MODEL), row), pl.BlockSpec((tm * ROW_SUBLANES, V7X_LANES), row),
                   pl.BlockSpec((tm, V7X_LANES), row), pl.BlockSpec((tm, V7X_LANES), row),
                   _full((V7X_SUBLANES, V7X_LANES))],
        out_shape=[jax.ShapeDtypeStruct((t, D_MODEL), _F32),
                   jax.ShapeDtypeStruct((t * ROW_SUBLANES, V7X_LANES), _F32),
                   jax.ShapeDtypeStruct((t, V7X_LANES), jnp.int32), jax.ShapeDtypeStruct((t, V7X_LANES), _F32),
                   jax.ShapeDtypeStruct((V7X_SUBLANES, V7X_LANES), _F32)],
        compiler_params=_params("arbitrary"),
        name="proj_router",
    )(x2d, *acts, *[w.astype(_BF16) for w in weights], norm_w.reshape(1, D_MODEL), wr_hi, wr_lo, br)
    counts = counts[0, MOE_GROUPS:MOE_GROUPS + MOE_EXPERTS].astype(jnp.int32)
    return x1, h, route, gate, counts


def _store_row_tiles(ref, val):
    n = val.shape[0]
    for s in range(ROW_SUBLANES):
        ref[pl.ds(s, n, stride=ROW_SUBLANES), :] = val[:, s * V7X_LANES:(s + 1) * V7X_LANES]


def _load_row_tiles(ref, n):
    return jnp.concatenate([ref[pl.ds(s, n, stride=ROW_SUBLANES), :] for s in range(ROW_SUBLANES)], axis=1)


def _expert_kernel(bexp_ref, nused_ref, gidx_ref, gnext_ref, sidx_ref, h_hbm, w1_ref, w3_ref, w2_ref, out_hbm,
                   xbuf, ybuf, w13_bf, w2_bf, sem_in, sem_out):
    i = pl.program_id(0)

    @pl.when((i == 0) | (bexp_ref[i] != bexp_ref[jnp.maximum(i - 1, 0)]))
    def _():
        w13_bf[:, :MOE_HIDDEN] = w1_ref[...].astype(_BF16)
        w13_bf[:, MOE_HIDDEN:] = w3_ref[...].astype(_BF16)
        w2_bf[...] = w2_ref[...].astype(_BF16)

    n_used = nused_ref[0]
    slot = lax.rem(i, 2)
    other = 1 - slot
    tile = ROW_SUBLANES

    def gather_copy(buf_slot, r, tok):
        return pltpu.make_async_copy(h_hbm.at[pl.ds(pl.multiple_of(tok * tile, tile), tile)],
                                     xbuf.at[buf_slot, pl.ds(pl.multiple_of(r * tile, tile), tile)],
                                     sem_in.at[buf_slot])

    def scatter_copy(buf_slot, r, row):
        return pltpu.make_async_copy(ybuf.at[buf_slot, pl.ds(pl.multiple_of(r * tile, tile), tile)],
                                     out_hbm.at[pl.ds(pl.multiple_of(row * tile, tile), tile)],
                                     sem_out.at[buf_slot])

    def for_rows(fn):
        def body(j, carry):
            for u in range(DMA_UNROLL):
                fn(j * DMA_UNROLL + u)
            return carry
        lax.fori_loop(0, MOE_ROWS // DMA_UNROLL, body, 0)

    def start_gather(idx_ref, buf_slot):
        for_rows(lambda r: gather_copy(buf_slot, r, idx_ref[0, r]).start())

    def wait_gather(buf_slot):
        for r in range(MOE_ROWS):
            gather_copy(buf_slot, r, 0).wait()

    def start_scatter(buf_slot):
        for_rows(lambda r: scatter_copy(buf_slot, r, sidx_ref[0, r]).start(priority=1))

    def wait_scatter(buf_slot):
        for r in range(MOE_ROWS):
            scatter_copy(buf_slot, r, 0).wait()

    @pl.when(i == 0)
    def _():
        spare0 = out_hbm.shape[0] - 2 * MOE_ROWS * tile
        ybuf[...] = jnp.zeros_like(ybuf)
        fills = [pltpu.make_async_copy(ybuf.at[s], out_hbm.at[pl.ds(spare0 + s * MOE_ROWS * tile, MOE_ROWS * tile)],
                                       sem_out.at[s]) for s in range(2)]
        for f in fills:
            f.start()
        for f in fills:
            f.wait()
        start_gather(gidx_ref, 0)

    @pl.when(i + 1 < n_used)
    def _():
        start_gather(gnext_ref, other)

    @pl.when(i < n_used)
    def _():
        wait_gather(slot)
        xb = _load_row_tiles(xbuf.at[slot], MOE_ROWS).astype(_BF16)
        a = jnp.dot(xb, w13_bf[...], preferred_element_type=_F32)
        hid = _silu(a[:, :MOE_HIDDEN]) * a[:, MOE_HIDDEN:]
        y = jnp.dot(hid.astype(_BF16), w2_bf[...], preferred_element_type=_F32)

        @pl.when(i >= 2)
        def _():
            wait_scatter(slot)

        _store_row_tiles(ybuf.at[slot], y)
        start_scatter(slot)

        @pl.when(i == n_used - 1)
        def _():
            wait_scatter(slot)

            @pl.when(i >= 1)
            def _():
                wait_scatter(other)


def _expert_plan(route, counts, n_tok):
    n_assign = n_tok * MOE_TOP_K
    n_blocks = n_assign // MOE_ROWS + MOE_EXPERTS
    n_slots = n_blocks * MOE_ROWS
    experts = jnp.arange(MOE_EXPERTS, dtype=jnp.int32)
    nblk = (counts + MOE_ROWS - 1) // MOE_ROWS
    blk_end = jnp.cumsum(nblk)
    blk_first = blk_end - nblk
    n_used = blk_end[-1]
    bid = jnp.arange(n_blocks, dtype=jnp.int32)
    bexp = jnp.minimum(jnp.sum(bid[:, None] >= blk_end[None, :], axis=1), MOE_EXPERTS - 1).astype(jnp.int32)
    last_exp = jnp.max(jnp.where(counts > 0, experts, 0))
    bexp = jnp.where(bid < n_used, bexp, last_exp)
    e = route[:, 0:MOE_TOP_K]
    rank = route[:, MOE_TOP_K:2 * MOE_TOP_K]
    first_slot = jnp.sum(jnp.where(e[:, :, None] == experts, blk_first * MOE_ROWS, 0), axis=-1)
    dest = (first_slot + rank).reshape(n_assign)
    tok = jnp.broadcast_to(jnp.arange(n_tok, dtype=jnp.int32)[:, None], (n_tok, MOE_TOP_K))
    row = tok + jnp.arange(MOE_TOP_K, dtype=jnp.int32) * n_tok
    r = jnp.arange(MOE_ROWS, dtype=jnp.int32)
    spare = (MOE_TOP_K * n_tok + (bid[:, None] % 2) * MOE_ROWS + r[None, :]).reshape(n_slots)
    spare_at_dest = MOE_TOP_K * n_tok + ((dest // MOE_ROWS) % 2) * MOE_ROWS + dest % MOE_ROWS
    gidx = jnp.zeros((n_slots,), jnp.int32).at[dest].add(tok.reshape(n_assign))
    sidx = spare + jnp.zeros((n_slots,), jnp.int32).at[dest].add(row.reshape(n_assign) - spare_at_dest)
    shape3 = (n_blocks, 1, MOE_ROWS)
    return bexp, n_used.reshape(1).astype(jnp.int32), gidx.reshape(shape3), sidx.reshape(shape3), n_blocks


def _experts(h_tiles, route, counts, w1, w3, w2, layer):
    n_tok = h_tiles.shape[0] // ROW_SUBLANES
    bexp, n_used, gidx, sidx, n_blocks = _expert_plan(route, counts, n_tok)
    idx_block = (None, 1, MOE_ROWS)
    per_expert = lambda i, bexp, nu: (layer, bexp[i], 0, 0)
    grid_spec = pltpu.PrefetchScalarGridSpec(
        num_scalar_prefetch=2,
        grid=(n_blocks,),
        in_specs=[pl.BlockSpec(idx_block, lambda i, bexp, nu: (i, 0, 0), memory_space=pltpu.SMEM),
                  pl.BlockSpec(idx_block, lambda i, bexp, nu: (jnp.minimum(i + 1, n_blocks - 1), 0, 0),
                               memory_space=pltpu.SMEM),
                  pl.BlockSpec(idx_block, lambda i, bexp, nu: (i, 0, 0), memory_space=pltpu.SMEM),
                  pl.BlockSpec(memory_space=pl.ANY),
                  pl.BlockSpec((None, None, D_MODEL, MOE_HIDDEN), per_expert),
                  pl.BlockSpec((None, None, D_MODEL, MOE_HIDDEN), per_expert),
                  pl.BlockSpec((None, None, MOE_HIDDEN, D_MODEL), per_expert)],
        out_specs=pl.BlockSpec(memory_space=pl.ANY),
        scratch_shapes=[pltpu.VMEM((2, MOE_ROWS * ROW_SUBLANES, V7X_LANES), _F32),
                        pltpu.VMEM((2, MOE_ROWS * ROW_SUBLANES, V7X_LANES), _F32),
                        pltpu.VMEM((D_MODEL, 2 * MOE_HIDDEN), _BF16), pltpu.VMEM((MOE_HIDDEN, D_MODEL), _BF16),
                        pltpu.SemaphoreType.DMA((2,)), pltpu.SemaphoreType.DMA((2,))],
    )
    out_rows = MOE_TOP_K * n_tok + 2 * MOE_ROWS
    return pl.pallas_call(
        _expert_kernel,
        grid_spec=grid_spec,
        out_shape=jax.ShapeDtypeStruct((out_rows * ROW_SUBLANES, V7X_LANES), _F32),
        compiler_params=_params("arbitrary"),
        name="experts",
    )(bexp, n_used, gidx, gidx, sidx, h_tiles, w1, w3, w2)


def _combine(x1_ref, y0_ref, y1_ref, gate_ref):
    gate = gate_ref[...]
    tm = x1_ref.shape[0]
    return x1_ref[...] + gate[:, 0:1] * _load_row_tiles(y0_ref, tm) + gate[:, 1:2] * _load_row_tiles(y1_ref, tm)


def _combine_kernel(x1_ref, y0_ref, y1_ref, gate_ref, o_ref):
    o_ref[...] = _combine(x1_ref, y0_ref, y1_ref, gate_ref)


def _combine_dft_kernel(x1_ref, y0_ref, y1_ref, gate_ref, nw_ref, cosc_ref, sinc_ref, x_ref, hc_ref, hs_ref):
    x = _combine(x1_ref, y0_ref, y1_ref, gate_ref)
    x_ref[...] = x
    h = _rms_rows(x, nw_ref[...]).astype(_BF16)
    for g in range(FOURIER_GROUPS):
        c0 = g * FOURIER_GROUP_DIM
        hg = h[:, c0:c0 + FOURIER_GROUP_DIM]
        hc_ref[:, c0:c0 + FOURIER_GROUP_DIM] = jnp.dot(hg, cosc_ref[...], preferred_element_type=_F32).astype(_BF16)
        hs_ref[:, c0:c0 + FOURIER_GROUP_DIM] = jnp.dot(hg, sinc_ref[...], preferred_element_type=_F32).astype(_BF16)


def _moe_combine(x1, y, gate, dft=None):
    t = x1.shape[0]
    tm = _tile(t, TOKEN_TILE)
    row = lambda i: (i, 0)
    second = lambda i: (i + t // tm, 0)
    y_block = (tm * ROW_SUBLANES, V7X_LANES)
    in_specs = [pl.BlockSpec((tm, D_MODEL), row), pl.BlockSpec(y_block, row),
                pl.BlockSpec(y_block, second), pl.BlockSpec((tm, V7X_LANES), row)]
    if dft is None:
        return pl.pallas_call(
            _combine_kernel, grid=(t // tm,), in_specs=in_specs,
            out_specs=pl.BlockSpec((tm, D_MODEL), row),
            out_shape=jax.ShapeDtypeStruct((t, D_MODEL), _F32),
            compiler_params=_params("parallel"), name="moe_combine",
        )(x1, y, y, gate)
    norm_w, cosc, sinc = dft
    return pl.pallas_call(
        _combine_dft_kernel, grid=(t // tm,),
        in_specs=in_specs + [_full((1, D_MODEL)), _full(cosc.shape), _full(sinc.shape)],
        out_specs=[pl.BlockSpec((tm, D_MODEL), row)] * 3,
        out_shape=[jax.ShapeDtypeStruct((t, D_MODEL), _F32), jax.ShapeDtypeStruct((t, D_MODEL), _BF16),
                   jax.ShapeDtypeStruct((t, D_MODEL), _BF16)],
        compiler_params=_params("parallel"), name="moe_combine_dft",
    )(x1, y, y, gate, norm_w.reshape(1, D_MODEL), cosc, sinc)


def _seq_dft_kernel(ca_ref, sa_ref, cb_ref, sb_ref, hc_ref, hs_ref, o_ref, cos_tab, nsin_tab, ec_ref, os_ref,
                    *, scale):
    seq, half = cos_tab.shape
    blk = DFT_REV_BLOCK
    n_blk = half // blk

    @pl.when(pl.program_id(0) == 0)
    def _():
        cb = cb_ref[...]
        sb = sb_ref[...]

        def gen(j1, carry):
            r0 = pl.multiple_of(j1 * DFT_FINE, DFT_FINE)
            ca = ca_ref[pl.ds(j1, 1), :]
            sa = sa_ref[pl.ds(j1, 1), :]
            cos_tab[pl.ds(r0, DFT_FINE), :] = (ca * cb - sa * sb).astype(_BF16)
            nsin_tab[pl.ds(r0, DFT_FINE), :] = (-(sa * cb + ca * sb)).astype(_BF16)
            return carry
        lax.fori_loop(0, seq // DFT_FINE, gen, 0)

    ri = lax.broadcasted_iota(jnp.int32, (blk, blk), 0)
    ci = lax.broadcasted_iota(jnp.int32, (blk, blk), 1)
    flip = (ci == blk - ri).astype(_BF16)
    row0 = lax.broadcasted_iota(jnp.int32, (blk, 1), 0) == 0

    def mirrored(ref, kb):
        lo = half + (n_blk - 1 - kb) * blk
        m = jnp.dot(flip, ref[lo:lo + blk, :], preferred_element_type=_F32)
        if kb >= 1:
            m = jnp.where(row0, ref[lo + blk:lo + blk + 1, :].astype(_F32), m)
        return m

    for kb in range(n_blk):
        rows = slice(kb * blk, (kb + 1) * blk)
        ec_ref[rows, :] = (hc_ref[rows, :].astype(_F32) + mirrored(hc_ref, kb)).astype(_BF16)
        os_ref[rows, :] = (hs_ref[rows, :].astype(_F32) - mirrored(hs_ref, kb)).astype(_BF16)

    mid = hc_ref[half:half + 1, :].astype(_F32) * scale
    tq = min(seq, DFT_ROW_TILE)

    def rows_out(t, carry):
        r0 = pl.multiple_of(t * tq, tq)
        j = r0 + lax.broadcasted_iota(jnp.int32, (tq, 1), 0)
        sign = (1 - 2 * (j & 1)).astype(_F32)
        acc = (jnp.dot(cos_tab[pl.ds(r0, tq), :], ec_ref[...], preferred_element_type=_F32)
               + jnp.dot(nsin_tab[pl.ds(r0, tq), :], os_ref[...], preferred_element_type=_F32))
        o_ref[pl.ds(r0, tq), :] = (acc + sign * mid).astype(_BF16)
        return carry
    lax.fori_loop(0, seq // tq, rows_out, 0)


def _seq_dft(hc, hs, batch, seq):
    half = seq // 2
    assert seq % DFT_FINE == 0 and half % DFT_REV_BLOCK == 0
    n_coarse = seq // DFT_FINE
    scale = seq ** -0.5
    ca, sa = _dft_tables(jnp.arange(n_coarse, dtype=jnp.int32) * DFT_FINE, seq, scale, half)
    cb, sb = _dft_tables(jnp.arange(DFT_FINE, dtype=jnp.int32), seq, 1.0, half)
    per_b = pl.BlockSpec((None, seq, D_MODEL), lambda b: (b, 0, 0))
    out = pl.pallas_call(
        functools.partial(_seq_dft_kernel, scale=scale), grid=(batch,),
        in_specs=[_full((n_coarse, half)), _full((n_coarse, half)), _full((DFT_FINE, half)), _full((DFT_FINE, half)),
                  per_b, per_b],
        out_specs=per_b,
        out_shape=jax.ShapeDtypeStruct((batch, seq, D_MODEL), _BF16),
        scratch_shapes=[pltpu.VMEM((seq, half), _BF16), pltpu.VMEM((seq, half), _BF16),
                        pltpu.VMEM((half, D_MODEL), _BF16), pltpu.VMEM((half, D_MODEL), _BF16)],
        compiler_params=_params("arbitrary"), name="seq_dft",
    )(ca, sa, cb, sb, hc.reshape(batch, seq, D_MODEL), hs.reshape(batch, seq, D_MODEL))
    return out.reshape(batch * seq, D_MODEL)


def _dft_tables(rows, n, scale, n_cols):
    k = jnp.arange(n_cols, dtype=jnp.int32)
    m = (rows[:, None] * k[None, :]) % n
    ang = m.astype(_F32) * (2.0 * math.pi / n)
    return jnp.cos(ang) * scale, jnp.sin(ang) * scale


def kernel(x, norm_mix, norm_ffn, w_in, q_norm, k_norm, lambda_q1, lambda_k1, lambda_q2, lambda_k2,
           attn_subln, conv_w, conv_b, dt_bias, a_log, d_skip, ssd_norm, w_out, w_fourier,
           router_group_w, router_group_b, router_expert_w, router_expert_b,
           expert_w1, expert_w3, expert_w2):
    batch, seq, _ = x.shape
    depth = norm_mix.shape[0]
    xt = x.reshape(batch * seq, D_MODEL)
    pending = None
    for l in range(depth):
        i = l // 2
        if l % 2 == 0:
            if pending is not None:
                xt = _moe_combine(*pending)
            lambda_init = 0.8 - 0.6 * math.exp(-0.3 * l)
            q, k, v, z, xbc, dt, dtt = _in_proj(xt, norm_mix[l], w_in[i], q_norm[i], k_norm[i])
            attn = _diff_attention(q, k, v, lambda_q1[i], lambda_k1[i], lambda_q2[i], lambda_k2[i],
                                   attn_subln[i], lambda_init, batch, seq)
            ssd = _ssd_mixer(z, xbc, dt, dtt, conv_w[i], conv_b[i], dt_bias[i], a_log[i], d_skip[i],
                             ssd_norm[i], batch, seq)
            acts = [attn, ssd]
            weights = [w_out[i][:ATTN_WIDTH], w_out[i][ATTN_WIDTH:]]
        else:
            cos_c, sin_c = _dft_tables(jnp.arange(FOURIER_GROUP_DIM, dtype=jnp.int32), FOURIER_GROUP_DIM,
                                       FOURIER_GROUP_DIM ** -0.5, FOURIER_GROUP_DIM)
            dft = (norm_mix[l], cos_c.astype(_BF16), sin_c.astype(_BF16))
            if pending is None:
                zeros = jnp.zeros((2 * batch * seq * ROW_SUBLANES, V7X_LANES), _F32)
                pending = (xt, zeros, jnp.zeros((batch * seq, V7X_LANES), _F32))
            xt, hc, hs = _moe_combine(*pending, dft=dft)
            f = _seq_dft(hc, hs, batch, seq)
            acts = [f]
            weights = [w_fourier[i]]
        x1, h, route, gate, counts = _proj_router(xt, acts, weights, norm_ffn[l], router_group_w[l],
                                                  router_group_b[l], router_expert_w[l], router_expert_b[l])
        y = _experts(h, route, counts, expert_w1, expert_w3, expert_w2, l)
        pending = (x1, y, gate)
    xt = _moe_combine(*pending)
    return xt.reshape(batch, seq, D_MODEL)
```

```python
import functools
import math

import jax
import jax.numpy as jnp
from jax import lax
from jax.experimental import pallas as pl
from jax.experimental.pallas import tpu as pltpu

D_MODEL = 1024
N_DIFF_HEADS = 4
DIFF_QK_DIM = 64
DIFF_V_DIM = 2 * DIFF_QK_DIM
ATTN_WIDTH = N_DIFF_HEADS * DIFF_V_DIM
SSD_HEAD_DIM = 64
SSD_WIDTH = D_MODEL // 2
SSD_HEADS = SSD_WIDTH // SSD_HEAD_DIM
SSD_GROUPS = 2
SSD_HEADS_PER_GROUP = SSD_HEADS // SSD_GROUPS
SSD_STATE = 64
SSD_CONV = 5
SSD_CHUNK = 128
SSD_BC = SSD_GROUPS * SSD_STATE
SSD_CONV_CH = SSD_WIDTH + 2 * SSD_BC
Q_COLS = N_DIFF_HEADS * 2 * DIFF_QK_DIM
DT_COLS = 2 * SSD_HEADS
FOURIER_GROUPS = 4
FOURIER_GROUP_DIM = D_MODEL // FOURIER_GROUPS
MOE_GROUPS = 4
MOE_EXPERTS_PER_GROUP = 8
MOE_EXPERTS = MOE_GROUPS * MOE_EXPERTS_PER_GROUP
MOE_TOP_K = 2
MOE_HIDDEN = 256
RMS_EPS = 1e-6
LOG2_E = math.log2(math.e)

V7X_LANES = 128
V7X_VMEM_LIMIT = 48 * 1024 * 1024
TOKEN_TILE = 512
ATTN_Q_TILE = 256
DFT_ROW_TILE = 512
MOE_ROWS = 256
V7X_SUBLANES = 8
ROW_SUBLANES = D_MODEL // V7X_LANES
DFT_FINE = 64
DFT_REV_BLOCK = 128
DMA_UNROLL = 16
CONV_ROWS = 128

_F32 = jnp.float32
_BF16 = jnp.bfloat16
_NT = (((1,), (1,)), ((), ()))
_TN = (((0,), (0,)), ((), ()))


def _params(*sem):
    return pltpu.CompilerParams(dimension_semantics=sem, vmem_limit_bytes=V7X_VMEM_LIMIT)


def _tile(n, want):
    t = min(n, want)
    assert n % t == 0, (n, t)
    return t


def _full(shape):
    return pl.BlockSpec(shape, lambda *_: (0,) * len(shape))


def _rms_rows(x, w):
    ms = jnp.mean(x * x, axis=-1, keepdims=True)
    return x * lax.rsqrt(ms + RMS_EPS) * w


def _silu(x):
    return x * jax.nn.sigmoid(x)


def _softplus(x):
    return jnp.maximum(x, 0.0) + jnp.log1p(jnp.exp(-jnp.abs(x)))


def _in_proj_kernel(x_ref, nw_ref, wqkv_ref, wz_ref, wxbc_ref, wdt_ref, wdtt_ref, qn_ref, kn_ref,
                    q_ref, k_ref, v_ref, z_ref, xbc_ref, dt_ref, dtt_ref):
    tm = x_ref.shape[0]
    h = _rms_rows(x_ref[...], nw_ref[...]).astype(_BF16)
    qkv = jnp.dot(h, wqkv_ref[...], preferred_element_type=_F32)
    low = lax.broadcasted_iota(jnp.int32, (tm, V7X_LANES), 1) < DIFF_QK_DIM

    def qk_norm(slab, w):
        sq = slab * slab
        s_lo = jnp.sum(jnp.where(low, sq, 0.0), axis=-1, keepdims=True)
        s_hi = jnp.sum(jnp.where(low, 0.0, sq), axis=-1, keepdims=True)
        ms = jnp.where(low, s_lo, s_hi) * (1.0 / DIFF_QK_DIM)
        return slab * lax.rsqrt(ms + RMS_EPS) * w

    scale = DIFF_QK_DIM ** -0.5 * LOG2_E
    for hd in range(N_DIFF_HEADS):
        c0 = hd * DIFF_V_DIM
        q_ref[:, c0:c0 + DIFF_V_DIM] = (qk_norm(qkv[:, c0:c0 + DIFF_V_DIM], qn_ref[...]) * scale).astype(_BF16)
        k_ref[:, c0:c0 + DIFF_V_DIM] = qk_norm(qkv[:, Q_COLS + c0:Q_COLS + c0 + DIFF_V_DIM], kn_ref[...]).astype(_BF16)
    v_ref[...] = qkv[:, 2 * Q_COLS:].astype(_BF16)
    z_ref[...] = jnp.dot(h, wz_ref[...], preferred_element_type=_F32).astype(_BF16)
    xbc_ref[...] = jnp.dot(h, wxbc_ref[...], preferred_element_type=_F32)
    dt_ref[...] = jnp.dot(h, wdt_ref[...], preferred_element_type=_F32)
    dtt = lax.dot_general(wdtt_ref[...], h, _NT, preferred_element_type=_F32)
    for j in range(tm // SSD_CHUNK):
        dtt_ref[j] = dtt[:, j * SSD_CHUNK:(j + 1) * SSD_CHUNK]


def _in_proj(x2d, norm_w, w_in, q_norm, k_norm):
    t = x2d.shape[0]
    tm = _tile(t, TOKEN_TILE)
    c_qkv = 3 * Q_COLS
    c_z = c_qkv + SSD_WIDTH
    c_xbc = c_z + SSD_CONV_CH
    wb = w_in.astype(_BF16)
    wqkv, wz, wxbc, wdt = wb[:, :c_qkv], wb[:, c_qkv:c_z], wb[:, c_z:c_xbc], wb[:, c_xbc:]
    row = lambda i: (i, 0)
    outs = pl.pallas_call(
        _in_proj_kernel,
        grid=(t // tm,),
        in_specs=[pl.BlockSpec((tm, D_MODEL), row), _full((1, D_MODEL)),
                  _full((D_MODEL, c_qkv)), _full((D_MODEL, SSD_WIDTH)), _full((D_MODEL, SSD_CONV_CH)),
                  _full((D_MODEL, DT_COLS)), _full((DT_COLS, D_MODEL)),
                  _full((1, DIFF_V_DIM)), _full((1, DIFF_V_DIM))],
        out_specs=[pl.BlockSpec((tm, Q_COLS), row), pl.BlockSpec((tm, Q_COLS), row),
                   pl.BlockSpec((tm, ATTN_WIDTH), row), pl.BlockSpec((tm, SSD_WIDTH), row),
                   pl.BlockSpec((tm, SSD_CONV_CH), row), pl.BlockSpec((tm, DT_COLS), row),
                   pl.BlockSpec((tm // SSD_CHUNK, DT_COLS, SSD_CHUNK), lambda i: (i, 0, 0))],
        out_shape=[jax.ShapeDtypeStruct((t, Q_COLS), _BF16), jax.ShapeDtypeStruct((t, Q_COLS), _BF16),
                   jax.ShapeDtypeStruct((t, ATTN_WIDTH), _BF16), jax.ShapeDtypeStruct((t, SSD_WIDTH), _BF16),
                   jax.ShapeDtypeStruct((t, SSD_CONV_CH), _F32), jax.ShapeDtypeStruct((t, DT_COLS), _F32),
                   jax.ShapeDtypeStruct((t // SSD_CHUNK, DT_COLS, SSD_CHUNK), _F32)],
        compiler_params=_params("parallel"),
        name="in_proj",
    )(x2d, norm_w.reshape(1, D_MODEL), wqkv, wz, wxbc, wdt, wdt.T,
      jnp.tile(q_norm, 2).reshape(1, DIFF_V_DIM), jnp.tile(k_norm, 2).reshape(1, DIFF_V_DIM))
    return outs


def _attn_kernel(slope_ref, lq1_ref, lk1_ref, lq2_ref, lk2_ref, q_ref, k_ref, v_ref, subln_ref, o_ref,
                 bias_ref, *, lambda_init):
    tq = q_ref.shape[0]
    seq = k_ref.shape[0]
    hd = pl.program_id(0)
    qi = pl.program_id(1)

    @pl.when(pl.program_id(2) == 0)
    def _():
        row = qi * tq + lax.broadcasted_iota(jnp.int32, (tq, seq), 0)
        col = lax.broadcasted_iota(jnp.int32, (tq, seq), 1)
        bias_ref[...] = (slope_ref[hd] * LOG2_E) * jnp.abs(row - col).astype(_F32)

    lam = (jnp.exp(jnp.sum(lq1_ref[...] * lk1_ref[...], keepdims=True))
           - jnp.exp(jnp.sum(lq2_ref[...] * lk2_ref[...], keepdims=True)) + lambda_init)
    q = q_ref[...]
    k = k_ref[...]
    low = lax.broadcasted_iota(jnp.int32, (tq, V7X_LANES), 1) < DIFF_QK_DIM
    zero = jnp.zeros_like(q)

    def branch(qc):
        s = lax.dot_general(qc, k, _NT, preferred_element_type=_F32) - bias_ref[...]
        p = jnp.exp2(s - jnp.max(s, axis=-1, keepdims=True))
        return p, jnp.sum(p, axis=-1, keepdims=True)

    p1, l1 = branch(jnp.where(low, q, zero))
    p2, l2 = branch(jnp.where(low, zero, q))
    w = p1 * (1.0 / l1) - p2 * (lam / l2)
    o = jnp.dot(w.astype(_BF16), v_ref[...], preferred_element_type=_F32)
    o_ref[...] = (_rms_rows(o, subln_ref[...]) * (1.0 - lambda_init)).astype(_BF16)


def _diff_attention(q, k, v, lq1, lk1, lq2, lk2, subln, lambda_init, batch, seq):
    tq = _tile(seq, ATTN_Q_TILE)
    slopes = jnp.asarray([2.0 ** (-8.0 * (h + 1) / N_DIFF_HEADS) for h in range(N_DIFF_HEADS)], _F32)
    vec = lambda a: a.reshape(1, -1).astype(_F32)
    q3, k3, v3 = (a.reshape(batch, seq, ATTN_WIDTH) for a in (q, k, v))
    kv_spec = pl.BlockSpec((None, seq, DIFF_V_DIM), lambda h, i, b: (b, 0, h))
    q_spec = pl.BlockSpec((None, tq, DIFF_V_DIM), lambda h, i, b: (b, i, h))
    out = pl.pallas_call(
        functools.partial(_attn_kernel, lambda_init=lambda_init),
        grid=(N_DIFF_HEADS, seq // tq, batch),
        in_specs=[pl.BlockSpec(memory_space=pltpu.SMEM)] + [_full((1, DIFF_QK_DIM))] * 4
                 + [q_spec, kv_spec, kv_spec, _full((1, DIFF_V_DIM))],
        out_specs=q_spec,
        out_shape=jax.ShapeDtypeStruct((batch, seq, ATTN_WIDTH), _BF16),
        scratch_shapes=[pltpu.VMEM((tq, seq), _F32)],
        compiler_params=_params("parallel", "parallel", "arbitrary"),
        name="diff_attn",
    )(slopes, vec(lq1), vec(lk1), vec(lq2), vec(lk2), q3, k3, v3, vec(subln))
    return out.reshape(batch * seq, ATTN_WIDTH)


def _ssd_kernel(xbc_ref, z_ref, dt_ref, dtt_ref, cw_ref, cb_ref, dtb_row_ref, dtb_col_ref,
                alog_row_ref, alog_col_ref, dskip_ref, nw_ref, o_ref,
                xpad_ref, xc_ref, y_ref, state_ref):
    seq = xbc_ref.shape[0]
    n_chunks = seq // SSD_CHUNK
    L = SSD_CHUNK
    halo = 8
    pad = SSD_CONV // 2

    xpad_ref[0:halo, :] = jnp.zeros((halo, SSD_CONV_CH), _F32)
    xpad_ref[halo + seq:2 * halo + seq, :] = jnp.zeros((halo, SSD_CONV_CH), _F32)
    xpad_ref[halo:halo + seq, :] = xbc_ref[...]
    for rb in range(seq // CONV_ROWS):
        base = rb * CONV_ROWS + halo - pad
        acc = jnp.broadcast_to(cb_ref[...], (CONV_ROWS, SSD_CONV_CH))
        for j in range(SSD_CONV):
            acc = acc + cw_ref[j:j + 1, :] * xpad_ref[base + j:base + j + CONV_ROWS, :]
        xc_ref[rb * CONV_ROWS:(rb + 1) * CONV_ROWS, :] = _silu(acc).astype(_BF16)

    sub = lax.broadcasted_iota(jnp.int32, (L, L), 0)
    lane = lax.broadcasted_iota(jnp.int32, (L, L), 1)
    causal = lane <= sub
    anti = lane >= sub
    tri = causal.astype(_BF16)
    trit = anti.astype(_BF16)
    neg_big = jnp.float32(-1e30)
    head_of_col = lax.broadcasted_iota(jnp.int32, (DT_COLS, SSD_WIDTH), 1) // SSD_HEAD_DIM
    dtcol = lax.broadcasted_iota(jnp.int32, (DT_COLS, SSD_WIDTH), 0)
    expand_f = (dtcol == head_of_col).astype(_BF16)
    expand_b = (dtcol == head_of_col + SSD_HEADS).astype(_BF16)
    state_row_group = lax.broadcasted_iota(jnp.int32, (SSD_BC, SSD_WIDTH), 0) // SSD_STATE
    state_col_group = (lax.broadcasted_iota(jnp.int32, (SSD_BC, SSD_WIDTH), 1)
                       // (SSD_HEAD_DIM * SSD_HEADS_PER_GROUP))
    state_mask = state_row_group == state_col_group
    a_row = -jnp.exp(alog_row_ref[...])
    a_col = -jnp.exp(alog_col_ref[...])

    def split(a, n):
        parts = []
        for _ in range(n - 1):
            p = a.astype(_BF16)
            parts.append(p)
            a = a - p.astype(_F32)
        return parts + [a.astype(_BF16)]

    def cumsum_cols(a):
        return sum(jnp.dot(tri, p, preferred_element_type=_F32) for p in split(a, 3))

    def cumsum_rows(a):
        return sum(jnp.dot(p, trit, preferred_element_type=_F32) for p in split(a, 3))

    def spread(v, expand):
        return sum(jnp.dot(p, expand, preferred_element_type=_F32) for p in split(v, 2))

    def load_chunk(c):
        r0 = pl.multiple_of(c * L, L)
        xs = xc_ref[pl.ds(r0, L), 0:SSD_WIDTH]
        bc = xc_ref[pl.ds(r0, L), SSD_WIDTH:SSD_WIDTH + SSD_BC]
        cc = xc_ref[pl.ds(r0, L), SSD_WIDTH + SSD_BC:SSD_CONV_CH]
        dtc = _softplus(dt_ref[pl.ds(r0, L), :] + dtb_row_ref[...])
        a_c = dtc * a_row
        acs_c = cumsum_cols(a_c)
        return r0, xs, bc, cc, dtc, a_c, acs_c

    def state_step(state, bc, xs, weight_cols, total_row, expand):
        wexp = spread(weight_cols, expand)
        xw = (xs.astype(_F32) * wexp).astype(_BF16)
        new = lax.dot_general(bc, xw, _TN, preferred_element_type=_F32)
        decay = jnp.exp(spread(jnp.broadcast_to(total_row, (8, DT_COLS)), expand))[0:1, :]
        return state * decay + jnp.where(state_mask, new, 0.0)

    state_ref[...] = jnp.zeros_like(state_ref)

    def fwd_body(c, carry):
        r0, xs, bc, cc, dtc, a_c, acs_c = load_chunk(c)
        dtr = _softplus(dtt_ref[c] + dtb_col_ref[...])
        a_r = dtr * a_col
        acs_r = cumsum_rows(a_r)
        ecs_c = acs_c - a_c
        ecs_r = acs_r - a_r
        lane_bc = lax.broadcasted_iota(jnp.int32, (L, SSD_BC), 1)
        gmat = []
        for g in range(SSD_GROUPS):
            in_g = (lane_bc >= g * SSD_STATE) & (lane_bc < (g + 1) * SSD_STATE)
            cg = jnp.where(in_g, cc, jnp.zeros_like(cc))
            gmat.append(lax.dot_general(cg, bc, _NT, preferred_element_type=_F32))
        parts = []
        for h in range(SSD_HEADS):
            hb = SSD_HEADS + h
            lf = jnp.exp(jnp.where(causal, acs_c[:, h:h + 1] - acs_r[h:h + 1, :], neg_big))
            lb = jnp.exp(jnp.where(anti, ecs_r[hb:hb + 1, :] - ecs_c[:, hb:hb + 1], neg_big))
            w = gmat[h // SSD_HEADS_PER_GROUP] * (lf * dtr[h:h + 1, :] + lb * dtr[hb:hb + 1, :])
            parts.append(jnp.dot(w.astype(_BF16), xs[:, h * SSD_HEAD_DIM:(h + 1) * SSD_HEAD_DIM],
                                 preferred_element_type=_F32))
        y_diag = jnp.concatenate(parts, axis=-1)
        state = state_ref[...]
        y_off = (jnp.dot(cc, state.astype(_BF16), preferred_element_type=_F32)
                 * spread(jnp.exp(acs_c), expand_f))
        y_ref[pl.ds(r0, L), :] = y_diag + y_off
        total = acs_c[L - 1:L, :]
        state_ref[...] = state_step(state, bc, xs, jnp.exp(total - acs_c) * dtc, total, expand_f)
        return carry

    lax.fori_loop(0, n_chunks, fwd_body, 0, unroll=4)

    state_ref[...] = jnp.zeros_like(state_ref)

    def bwd_body(i, carry):
        c = n_chunks - 1 - i
        r0, xs, bc, cc, dtc, a_c, acs_c = load_chunk(c)
        ecs_c = acs_c - a_c
        total = acs_c[L - 1:L, :]
        state = state_ref[...]
        y_off = (jnp.dot(cc, state.astype(_BF16), preferred_element_type=_F32)
                 * spread(jnp.exp(total - ecs_c), expand_b))
        y = y_ref[pl.ds(r0, L), :] + y_off + dskip_ref[...] * xs.astype(_F32)
        gated = y * _silu(z_ref[pl.ds(r0, L), :].astype(_F32))
        o_ref[pl.ds(r0, L), :] = _rms_rows(gated, nw_ref[...]).astype(_BF16)
        state_ref[...] = state_step(state, bc, xs, jnp.exp(ecs_c) * dtc, total, expand_b)
        return carry

    lax.fori_loop(0, n_chunks, bwd_body, 0, unroll=4)


def _ssd_mixer(z, xbc, dt, dtt, conv_w, conv_b, dt_bias, a_log, d_skip, norm_w, batch, seq):
    n_chunks = seq // SSD_CHUNK
    per_b = lambda w: pl.BlockSpec((None, seq, w), lambda b: (b, 0, 0))
    dtb = dt_bias.reshape(DT_COLS).astype(_F32)
    alog = a_log.reshape(DT_COLS).astype(_F32)
    out = pl.pallas_call(
        _ssd_kernel,
        grid=(batch,),
        in_specs=[per_b(SSD_CONV_CH), per_b(SSD_WIDTH), per_b(DT_COLS),
                  pl.BlockSpec((n_chunks, DT_COLS, SSD_CHUNK), lambda b: (b, 0, 0)),
                  _full((SSD_CONV, SSD_CONV_CH)), _full((1, SSD_CONV_CH)),
                  _full((1, DT_COLS)), _full((DT_COLS, 1)), _full((1, DT_COLS)), _full((DT_COLS, 1)),
                  _full((1, SSD_WIDTH)), _full((1, SSD_WIDTH))],
        out_specs=per_b(SSD_WIDTH),
        out_shape=jax.ShapeDtypeStruct((batch, seq, SSD_WIDTH), _BF16),
        scratch_shapes=[pltpu.VMEM((seq + 16, SSD_CONV_CH), _F32), pltpu.VMEM((seq, SSD_CONV_CH), _BF16),
                        pltpu.VMEM((seq, SSD_WIDTH), _F32), pltpu.VMEM((SSD_BC, SSD_WIDTH), _F32)],
        compiler_params=_params("parallel"),
        name="ssd_mixer",
    )(xbc.reshape(batch, seq, SSD_CONV_CH), z.reshape(batch, seq, SSD_WIDTH), dt.reshape(batch, seq, DT_COLS),
      dtt, conv_w.astype(_F32), conv_b.reshape(1, SSD_CONV_CH).astype(_F32),
      dtb.reshape(1, DT_COLS), dtb.reshape(DT_COLS, 1), alog.reshape(1, DT_COLS), alog.reshape(DT_COLS, 1),
      jnp.repeat(d_skip.astype(_F32), SSD_HEAD_DIM).reshape(1, SSD_WIDTH), norm_w.reshape(1, SSD_WIDTH))
    return out.reshape(batch * seq, SSD_WIDTH)


def _proj_router_kernel(*refs, n_act):
    x_ref = refs[0]
    acts = refs[1:1 + n_act]
    ws = refs[1 + n_act:1 + 2 * n_act]
    nw_ref, wr_hi_ref, wr_lo_ref, br_ref, x1_ref, h_ref, route_ref, gate_ref, count_ref = refs[1 + 2 * n_act:]
    tm = x_ref.shape[0]
    x1 = x_ref[...]
    for a_ref, w_ref in zip(acts, ws):
        x1 = x1 + jnp.dot(a_ref[...], w_ref[...], preferred_element_type=_F32)
    x1_ref[...] = x1
    h = _rms_rows(x1, nw_ref[...])
    _store_row_tiles(h_ref, h)

    h_hi = h.astype(_BF16)
    h_lo = (h - h_hi.astype(_F32)).astype(_BF16)
    logits = (jnp.dot(h_hi, wr_hi_ref[...], preferred_element_type=_F32)
              + jnp.dot(h_lo, wr_hi_ref[...], preferred_element_type=_F32)
              + jnp.dot(h_hi, wr_lo_ref[...], preferred_element_type=_F32) + br_ref[...])
    lane = lax.broadcasted_iota(jnp.int32, (tm, V7X_LANES), 1)
    lane_f = lane.astype(_F32)
    ninf = jnp.float32(-jnp.inf)
    nolane = jnp.float32(V7X_LANES)

    def argmax_first(vals):
        m = jnp.max(vals, axis=-1, keepdims=True)
        idx = jnp.min(jnp.where(vals == m, lane_f, nolane), axis=-1, keepdims=True)
        return m, idx

    gl = jnp.where(lane < MOE_GROUPS, logits, ninf)
    gmax, gidx = argmax_first(gl)
    g_p = 1.0 / jnp.sum(jnp.exp(gl - gmax), axis=-1, keepdims=True)
    first = MOE_GROUPS + MOE_EXPERTS_PER_GROUP * gidx
    el = jnp.where((lane_f >= first) & (lane_f < first + MOE_EXPERTS_PER_GROUP), logits, ninf)
    e1, i1 = argmax_first(el)
    e2, i2 = argmax_first(jnp.where(lane_f == i1, ninf, el))
    r = jnp.exp(e2 - e1)
    p1 = 1.0 / (1.0 + r)
    gate_ref[...] = jnp.where(lane == 0, g_p * p1, jnp.where(lane == 1, g_p * (r * p1), 0.0))

    @pl.when(pl.program_id(0) == 0)
    def _():
        count_ref[...] = jnp.zeros_like(count_ref)

    chosen = (lane_f == i1) | (lane_f == i2)
    onehot = jnp.where(chosen, 1.0, 0.0)
    earlier = (lax.broadcasted_iota(jnp.int32, (tm, tm), 1)
               < lax.broadcasted_iota(jnp.int32, (tm, tm), 0)).astype(_BF16)
    prior = jnp.dot(earlier, onehot.astype(_BF16), preferred_element_type=_F32) + count_ref[0:1, :]
    rank1 = jnp.sum(jnp.where(lane_f == i1, prior, 0.0), axis=-1, keepdims=True)
    rank2 = jnp.sum(jnp.where(lane_f == i2, prior, 0.0), axis=-1, keepdims=True)
    count_ref[...] = count_ref[...] + jnp.sum(onehot, axis=0, keepdims=True)
    route = jnp.where(lane == 0, i1 - MOE_GROUPS, jnp.where(lane == 1, i2 - MOE_GROUPS,
                      jnp.where(lane == 2, rank1, jnp.where(lane == 3, rank2, 0.0))))
    route_ref[...] = route.astype(jnp.int32)


def _proj_router(x2d, acts, weights, norm_w, rg_w, rg_b, re_w, re_b):
    t = x2d.shape[0]
    tm = _tile(t, TOKEN_TILE)
    n_act = len(acts)
    n_log = MOE_GROUPS + MOE_EXPERTS
    wr = jnp.zeros((D_MODEL, V7X_LANES), _F32).at[:, :n_log].set(
        jnp.concatenate([rg_w.astype(_F32), re_w.reshape(D_MODEL, MOE_EXPERTS).astype(_F32)], axis=1))
    br = jnp.zeros((1, V7X_LANES), _F32).at[0, :n_log].set(
        jnp.concatenate([rg_b.astype(_F32), re_b.reshape(MOE_EXPERTS).astype(_F32)]))
    wr_hi = wr.astype(_BF16)
    wr_lo = (wr - wr_hi.astype(_F32)).astype(_BF16)
    row = lambda i: (i, 0)
    x1, h, route, gate, counts = pl.pallas_call(
        functools.partial(_proj_router_kernel, n_act=n_act),
        grid=(t // tm,),
        in_specs=[pl.BlockSpec((tm, D_MODEL), row)]
                 + [pl.BlockSpec((tm, a.shape[1]), row) for a in acts]
                 + [_full(w.shape) for w in weights]
                 + [_full((1, D_MODEL)), _full((D_MODEL, V7X_LANES)), _full((D_MODEL, V7X_LANES)),
                    _full((1, V7X_LANES))],
        out_specs=[pl.BlockSpec((tm, D_MODEL), row), pl.BlockSpec((tm * ROW_SUBLANES, V7X_LANES), row),
                   pl.BlockSpec((tm, V7X_LANES), row), pl.BlockSpec((tm, V7X_LANES), row),
                   _full((V7X_SUBLANES, V7X_LANES))],
        out_shape=[jax.ShapeDtypeStruct((t, D_MODEL), _F32),
                   jax.ShapeDtypeStruct((t * ROW_SUBLANES, V7X_LANES), _F32),
                   jax.ShapeDtypeStruct((t, V7X_LANES), jnp.int32), jax.ShapeDtypeStruct((t, V7X_LANES), _F32),
                   jax.ShapeDtypeStruct((V7X_SUBLANES, V7X_LANES), _F32)],
        compiler_params=_params("arbitrary"),
        name="proj_router",
    )(x2d, *acts, *[w.astype(_BF16) for w in weights], norm_w.reshape(1, D_MODEL), wr_hi, wr_lo, br)
    counts = counts[0, MOE_GROUPS:MOE_GROUPS + MOE_EXPERTS].astype(jnp.int32)
    return x1, h, route, gate, counts


def _store_row_tiles(ref, val):
    n = val.shape[0]
    for s in range(ROW_SUBLANES):
        ref[pl.ds(s, n, stride=ROW_SUBLANES), :] = val[:, s * V7X_LANES:(s + 1) * V7X_LANES]


def _load_row_tiles(ref, n):
    return jnp.concatenate([ref[pl.ds(s, n, stride=ROW_SUBLANES), :] for s in range(ROW_SUBLANES)], axis=1)


def _expert_kernel(bexp_ref, nused_ref, gidx_ref, gnext_ref, sidx_ref, h_hbm, w1_ref, w3_ref, w2_ref, out_hbm,
                   xbuf, ybuf, w13_bf, w2_bf, sem_in, sem_out):
    i = pl.program_id(0)

    @pl.when((i == 0) | (bexp_ref[i] != bexp_ref[jnp.maximum(i - 1, 0)]))
    def _():
        w13_bf[:, :MOE_HIDDEN] = w1_ref[...].astype(_BF16)
        w13_bf[:, MOE_HIDDEN:] = w3_ref[...].astype(_BF16)
        w2_bf[...] = w2_ref[...].astype(_BF16)

    n_used = nused_ref[0]
    slot = lax.rem(i, 2)
    other = 1 - slot
    tile = ROW_SUBLANES

    def gather_copy(buf_slot, r, tok):
        return pltpu.make_async_copy(h_hbm.at[pl.ds(pl.multiple_of(tok * tile, tile), tile)],
                                     xbuf.at[buf_slot, pl.ds(pl.multiple_of(r * tile, tile), tile)],
                                     sem_in.at[buf_slot])

    def scatter_copy(buf_slot, r, row):
        return pltpu.make_async_copy(ybuf.at[buf_slot, pl.ds(pl.multiple_of(r * tile, tile), tile)],
                                     out_hbm.at[pl.ds(pl.multiple_of(row * tile, tile), tile)],
                                     sem_out.at[buf_slot])

    def for_rows(fn):
        def body(j, carry):
            for u in range(DMA_UNROLL):
                fn(j * DMA_UNROLL + u)
            return carry
        lax.fori_loop(0, MOE_ROWS // DMA_UNROLL, body, 0)

    def start_gather(idx_ref, buf_slot):
        for_rows(lambda r: gather_copy(buf_slot, r, idx_ref[0, r]).start())

    def wait_gather(buf_slot):
        for r in range(MOE_ROWS):
            gather_copy(buf_slot, r, 0).wait()

    def start_scatter(buf_slot):
        for_rows(lambda r: scatter_copy(buf_slot, r, sidx_ref[0, r]).start(priority=1))

    def wait_scatter(buf_slot):
        for r in range(MOE_ROWS):
            scatter_copy(buf_slot, r, 0).wait()

    @pl.when(i == 0)
    def _():
        spare0 = out_hbm.shape[0] - 2 * MOE_ROWS * tile
        ybuf[...] = jnp.zeros_like(ybuf)
        fills = [pltpu.make_async_copy(ybuf.at[s], out_hbm.at[pl.ds(spare0 + s * MOE_ROWS * tile, MOE_ROWS * tile)],
                                       sem_out.at[s]) for s in range(2)]
        for f in fills:
            f.start()
        for f in fills:
            f.wait()
        start_gather(gidx_ref, 0)

    @pl.when(i + 1 < n_used)
    def _():
        start_gather(gnext_ref, other)

    @pl.when(i < n_used)
    def _():
        wait_gather(slot)
        xb = _load_row_tiles(xbuf.at[slot], MOE_ROWS).astype(_BF16)
        a = jnp.dot(xb, w13_bf[...], preferred_element_type=_F32)
        hid = _silu(a[:, :MOE_HIDDEN]) * a[:, MOE_HIDDEN:]
        y = jnp.dot(hid.astype(_BF16), w2_bf[...], preferred_element_type=_F32)

        @pl.when(i >= 2)
        def _():
            wait_scatter(slot)

        _store_row_tiles(ybuf.at[slot], y)
        start_scatter(slot)

        @pl.when(i == n_used - 1)
        def _():
            wait_scatter(slot)

            @pl.when(i >= 1)
            def _():
                wait_scatter(other)


def _expert_plan(route, counts, n_tok):
    n_assign = n_tok * MOE_TOP_K
    n_blocks = n_assign // MOE_ROWS + MOE_EXPERTS
    n_slots = n_blocks * MOE_ROWS
    experts = jnp.arange(MOE_EXPERTS, dtype=jnp.int32)
    nblk = (counts + MOE_ROWS - 1) // MOE_ROWS
    blk_end = jnp.cumsum(nblk)
    blk_first = blk_end - nblk
    n_used = blk_end[-1]
    bid = jnp.arange(n_blocks, dtype=jnp.int32)
    bexp = jnp.minimum(jnp.sum(bid[:, None] >= blk_end[None, :], axis=1), MOE_EXPERTS - 1).astype(jnp.int32)
    last_exp = jnp.max(jnp.where(counts > 0, experts, 0))
    bexp = jnp.where(bid < n_used, bexp, last_exp)
    e = route[:, 0:MOE_TOP_K]
    rank = route[:, MOE_TOP_K:2 * MOE_TOP_K]
    first_slot = jnp.sum(jnp.where(e[:, :, None] == experts, blk_first * MOE_ROWS, 0), axis=-1)
    dest = (first_slot + rank).reshape(n_assign)
    tok = jnp.broadcast_to(jnp.arange(n_tok, dtype=jnp.int32)[:, None], (n_tok, MOE_TOP_K))
    row = tok + jnp.arange(MOE_TOP_K, dtype=jnp.int32) * n_tok
    r = jnp.arange(MOE_ROWS, dtype=jnp.int32)
    spare = (MOE_TOP_K * n_tok + (bid[:, None] % 2) * MOE_ROWS + r[None, :]).reshape(n_slots)
    spare_at_dest = MOE_TOP_K * n_tok + ((dest // MOE_ROWS) % 2) * MOE_ROWS + dest % MOE_ROWS
    gidx = jnp.zeros((n_slots,), jnp.int32).at[dest].add(tok.reshape(n_assign))
    sidx = spare + jnp.zeros((n_slots,), jnp.int32).at[dest].add(row.reshape(n_assign) - spare_at_dest)
    shape3 = (n_blocks, 1, MOE_ROWS)
    return bexp, n_used.reshape(1).astype(jnp.int32), gidx.reshape(shape3), sidx.reshape(shape3), n_blocks


def _experts(h_tiles, route, counts, w1, w3, w2, layer):
    n_tok = h_tiles.shape[0] // ROW_SUBLANES
    bexp, n_used, gidx, sidx, n_blocks = _expert_plan(route, counts, n_tok)
    idx_block = (None, 1, MOE_ROWS)
    per_expert = lambda i, bexp, nu: (layer, bexp[i], 0, 0)
    grid_spec = pltpu.PrefetchScalarGridSpec(
        num_scalar_prefetch=2,
        grid=(n_blocks,),
        in_specs=[pl.BlockSpec(idx_block, lambda i, bexp, nu: (i, 0, 0), memory_space=pltpu.SMEM),
                  pl.BlockSpec(idx_block, lambda i, bexp, nu: (jnp.minimum(i + 1, n_blocks - 1), 0, 0),
                               memory_space=pltpu.SMEM),
                  pl.BlockSpec(idx_block, lambda i, bexp, nu: (i, 0, 0), memory_space=pltpu.SMEM),
                  pl.BlockSpec(memory_space=pl.ANY),
                  pl.BlockSpec((None, None, D_MODEL, MOE_HIDDEN), per_expert),
                  pl.BlockSpec((None, None, D_MODEL, MOE_HIDDEN), per_expert),
                  pl.BlockSpec((None, None, MOE_HIDDEN, D_MODEL), per_expert)],
        out_specs=pl.BlockSpec(memory_space=pl.ANY),
        scratch_shapes=[pltpu.VMEM((2, MOE_ROWS * ROW_SUBLANES, V7X_LANES), _F32),
                        pltpu.VMEM((2, MOE_ROWS * ROW_SUBLANES, V7X_LANES), _F32),
                        pltpu.VMEM((D_MODEL, 2 * MOE_HIDDEN), _BF16), pltpu.VMEM((MOE_HIDDEN, D_MODEL), _BF16),
                        pltpu.SemaphoreType.DMA((2,)), pltpu.SemaphoreType.DMA((2,))],
    )
    out_rows = MOE_TOP_K * n_tok + 2 * MOE_ROWS
    return pl.pallas_call(
        _expert_kernel,
        grid_spec=grid_spec,
        out_shape=jax.ShapeDtypeStruct((out_rows * ROW_SUBLANES, V7X_LANES), _F32),
        compiler_params=_params("arbitrary"),
        name="experts",
    )(bexp, n_used, gidx, gidx, sidx, h_tiles, w1, w3, w2)


def _combine(x1_ref, y0_ref, y1_ref, gate_ref):
    gate = gate_ref[...]
    tm = x1_ref.shape[0]
    return x1_ref[...] + gate[:, 0:1] * _load_row_tiles(y0_ref, tm) + gate[:, 1:2] * _load_row_tiles(y1_ref, tm)


def _combine_kernel(x1_ref, y0_ref, y1_ref, gate_ref, o_ref):
    o_ref[...] = _combine(x1_ref, y0_ref, y1_ref, gate_ref)


def _combine_dft_kernel(x1_ref, y0_ref, y1_ref, gate_ref, nw_ref, cosc_ref, sinc_ref, x_ref, hc_ref, hs_ref):
    x = _combine(x1_ref, y0_ref, y1_ref, gate_ref)
    x_ref[...] = x
    h = _rms_rows(x, nw_ref[...]).astype(_BF16)
    for g in range(FOURIER_GROUPS):
        c0 = g * FOURIER_GROUP_DIM
        hg = h[:, c0:c0 + FOURIER_GROUP_DIM]
        hc_ref[:, c0:c0 + FOURIER_GROUP_DIM] = jnp.dot(hg, cosc_ref[...], preferred_element_type=_F32).astype(_BF16)
        hs_ref[:, c0:c0 + FOURIER_GROUP_DIM] = jnp.dot(hg, sinc_ref[...], preferred_element_type=_F32).astype(_BF16)


def _moe_combine(x1, y, gate, dft=None):
    t = x1.shape[0]
    tm = _tile(t, TOKEN_TILE)
    row = lambda i: (i, 0)
    second = lambda i: (i + t // tm, 0)
    y_block = (tm * ROW_SUBLANES, V7X_LANES)
    in_specs = [pl.BlockSpec((tm, D_MODEL), row), pl.BlockSpec(y_block, row),
                pl.BlockSpec(y_block, second), pl.BlockSpec((tm, V7X_LANES), row)]
    if dft is None:
        return pl.pallas_call(
            _combine_kernel, grid=(t // tm,), in_specs=in_specs,
            out_specs=pl.BlockSpec((tm, D_MODEL), row),
            out_shape=jax.ShapeDtypeStruct((t, D_MODEL), _F32),
            compiler_params=_params("parallel"), name="moe_combine",
        )(x1, y, y, gate)
    norm_w, cosc, sinc = dft
    return pl.pallas_call(
        _combine_dft_kernel, grid=(t // tm,),
        in_specs=in_specs + [_full((1, D_MODEL)), _full(cosc.shape), _full(sinc.shape)],
        out_specs=[pl.BlockSpec((tm, D_MODEL), row)] * 3,
        out_shape=[jax.ShapeDtypeStruct((t, D_MODEL), _F32), jax.ShapeDtypeStruct((t, D_MODEL), _BF16),
                   jax.ShapeDtypeStruct((t, D_MODEL), _BF16)],
        compiler_params=_params("parallel"), name="moe_combine_dft",
    )(x1, y, y, gate, norm_w.reshape(1, D_MODEL), cosc, sinc)


def _seq_dft_kernel(ca_ref, sa_ref, cb_ref, sb_ref, hc_ref, hs_ref, o_ref, cos_tab, nsin_tab, ec_ref, os_ref,
                    b_ref, *, scale):
    seq = hc_ref.shape[0]
    half = seq // 2
    blk = DFT_REV_BLOCK
    n_blk = half // blk

    @pl.when(pl.program_id(0) == 0)
    def _():
        cb = cb_ref[...]
        sb = sb_ref[...]

        def gen(j1, carry):
            r0 = pl.multiple_of(j1 * DFT_FINE, DFT_FINE)
            ca = ca_ref[pl.ds(j1, 1), :]
            sa = sa_ref[pl.ds(j1, 1), :]
            cos_tab[pl.ds(r0, DFT_FINE), :] = (ca * cb - sa * sb).astype(_BF16)
            nsin_tab[pl.ds(r0, DFT_FINE), :] = (-(sa * cb + ca * sb)).astype(_BF16)
            return carry
        lax.fori_loop(0, cos_tab.shape[0] // DFT_FINE, gen, 0)

    ri = lax.broadcasted_iota(jnp.int32, (blk, blk), 0)
    ci = lax.broadcasted_iota(jnp.int32, (blk, blk), 1)
    flip = (ci == blk - ri).astype(_BF16)
    row0 = lax.broadcasted_iota(jnp.int32, (blk, 1), 0) == 0

    def mirrored(ref, kb):
        lo = half + (n_blk - 1 - kb) * blk
        m = jnp.dot(flip, ref[lo:lo + blk, :], preferred_element_type=_F32)
        if kb >= 1:
            m = jnp.where(row0, ref[lo + blk:lo + blk + 1, :].astype(_F32), m)
        return m

    for kb in range(n_blk):
        rows = slice(kb * blk, (kb + 1) * blk)
        ec_ref[rows, :] = (hc_ref[rows, :].astype(_F32) + mirrored(hc_ref, kb)).astype(_BF16)
        os_ref[rows, :] = (hs_ref[rows, :].astype(_F32) - mirrored(hs_ref, kb)).astype(_BF16)

    mid = hc_ref[half:half + 1, :].astype(_F32) * scale
    tq = min(half, DFT_ROW_TILE)

    def rows_out(t, carry):
        r0 = pl.multiple_of(t * tq, tq)
        j = r0 + lax.broadcasted_iota(jnp.int32, (tq, 1), 0)
        base = (jnp.dot(cos_tab[pl.ds(r0, tq), :], ec_ref[...], preferred_element_type=_F32)
                + (1 - 2 * (j & 1)).astype(_F32) * mid)
        nq = jnp.dot(nsin_tab[pl.ds(r0, tq), :], os_ref[...], preferred_element_type=_F32)
        o_ref[pl.ds(r0, tq), :] = (base + nq).astype(_BF16)
        b_ref[pl.ds(r0, tq), :] = (base - nq).astype(_BF16)
        return carry
    lax.fori_loop(0, half // tq, rows_out, 0)

    sign_half = 1.0 - 2.0 * (half % 2)
    row_half = (jnp.dot(cos_tab[half:half + V7X_SUBLANES, :], ec_ref[...], preferred_element_type=_F32)[0:1, :]
                + sign_half * mid)
    for db in range(n_blk):
        src = (n_blk - 1 - db) * blk
        m = jnp.dot(flip, b_ref[src:src + blk, :], preferred_element_type=_F32)
        first = b_ref[src + blk:src + blk + 1, :].astype(_F32) if db >= 1 else row_half
        o_ref[half + db * blk:half + (db + 1) * blk, :] = jnp.where(row0, first, m).astype(_BF16)


def _seq_dft(hc, hs, batch, seq):
    half = seq // 2
    assert seq % DFT_FINE == 0 and half % DFT_REV_BLOCK == 0
    tab_rows = half + DFT_FINE
    n_coarse = tab_rows // DFT_FINE
    scale = seq ** -0.5
    ca, sa = _dft_tables(jnp.arange(n_coarse, dtype=jnp.int32) * DFT_FINE, seq, scale, half)
    cb, sb = _dft_tables(jnp.arange(DFT_FINE, dtype=jnp.int32), seq, 1.0, half)
    per_b = pl.BlockSpec((None, seq, D_MODEL), lambda b: (b, 0, 0))
    out = pl.pallas_call(
        functools.partial(_seq_dft_kernel, scale=scale), grid=(batch,),
        in_specs=[_full((n_coarse, half)), _full((n_coarse, half)), _full((DFT_FINE, half)), _full((DFT_FINE, half)),
                  per_b, per_b],
        out_specs=per_b,
        out_shape=jax.ShapeDtypeStruct((batch, seq, D_MODEL), _BF16),
        scratch_shapes=[pltpu.VMEM((tab_rows, half), _BF16), pltpu.VMEM((tab_rows, half), _BF16),
                        pltpu.VMEM((half, D_MODEL), _BF16), pltpu.VMEM((half, D_MODEL), _BF16),
                        pltpu.VMEM((half, D_MODEL), _BF16)],
        compiler_params=_params("arbitrary"), name="seq_dft",
    )(ca, sa, cb, sb, hc.reshape(batch, seq, D_MODEL), hs.reshape(batch, seq, D_MODEL))
    return out.reshape(batch * seq, D_MODEL)


def _dft_tables(rows, n, scale, n_cols):
    k = jnp.arange(n_cols, dtype=jnp.int32)
    m = (rows[:, None] * k[None, :]) % n
    ang = m.astype(_F32) * (2.0 * math.pi / n)
    return jnp.cos(ang) * scale, jnp.sin(ang) * scale


def kernel(x, norm_mix, norm_ffn, w_in, q_norm, k_norm, lambda_q1, lambda_k1, lambda_q2, lambda_k2,
           attn_subln, conv_w, conv_b, dt_bias, a_log, d_skip, ssd_norm, w_out, w_fourier,
           router_group_w, router_group_b, router_expert_w, router_expert_b,
           expert_w1, expert_w3, expert_w2):
    batch, seq, _ = x.shape
    depth = norm_mix.shape[0]
    xt = x.reshape(batch * seq, D_MODEL)
    pending = None
    for l in range(depth):
        i = l // 2
        if l % 2 == 0:
            if pending is not None:
                xt = _moe_combine(*pending)
            lambda_init = 0.8 - 0.6 * math.exp(-0.3 * l)
            q, k, v, z, xbc, dt, dtt = _in_proj(xt, norm_mix[l], w_in[i], q_norm[i], k_norm[i])
            attn = _diff_attention(q, k, v, lambda_q1[i], lambda_k1[i], lambda_q2[i], lambda_k2[i],
                                   attn_subln[i], lambda_init, batch, seq)
            ssd = _ssd_mixer(z, xbc, dt, dtt, conv_w[i], conv_b[i], dt_bias[i], a_log[i], d_skip[i],
                             ssd_norm[i], batch, seq)
            acts = [attn, ssd]
            weights = [w_out[i][:ATTN_WIDTH], w_out[i][ATTN_WIDTH:]]
        else:
            cos_c, sin_c = _dft_tables(jnp.arange(FOURIER_GROUP_DIM, dtype=jnp.int32), FOURIER_GROUP_DIM,
                                       FOURIER_GROUP_DIM ** -0.5, FOURIER_GROUP_DIM)
            dft = (norm_mix[l], cos_c.astype(_BF16), sin_c.astype(_BF16))
            if pending is None:
                zeros = jnp.zeros((2 * batch * seq * ROW_SUBLANES, V7X_LANES), _F32)
                pending = (xt, zeros, jnp.zeros((batch * seq, V7X_LANES), _F32))
            xt, hc, hs = _moe_combine(*pending, dft=dft)
            f = _seq_dft(hc, hs, batch, seq)
            acts = [f]
            weights = [w_fourier[i]]
        x1, h, route, gate, counts = _proj_router(xt, acts, weights, norm_ffn[l], router_group_w[l],
                                                  router_group_b[l], router_expert_w[l], router_expert_b[l])
        y = _experts(h, route, counts, expert_w1, expert_w3, expert_w2, l)
        pending = (x1, y, gate)
    xt = _moe_combine(*pending)
    return xt.reshape(batch, seq, D_MODEL)
```

```python
import functools
import math

import jax
import jax.numpy as jnp
from jax import lax
from jax.experimental import pallas as pl
from jax.experimental.pallas import tpu as pltpu

D_MODEL = 1024
N_DIFF_HEADS = 4
DIFF_QK_DIM = 64
DIFF_V_DIM = 2 * DIFF_QK_DIM
ATTN_WIDTH = N_DIFF_HEADS * DIFF_V_DIM
SSD_HEAD_DIM = 64
SSD_WIDTH = D_MODEL // 2
SSD_HEADS = SSD_WIDTH // SSD_HEAD_DIM
SSD_GROUPS = 2
SSD_HEADS_PER_GROUP = SSD_HEADS // SSD_GROUPS
SSD_STATE = 64
SSD_CONV = 5
SSD_CHUNK = 128
SSD_BC = SSD_GROUPS * SSD_STATE
SSD_CONV_CH = SSD_WIDTH + 2 * SSD_BC
Q_COLS = N_DIFF_HEADS * 2 * DIFF_QK_DIM
DT_COLS = 2 * SSD_HEADS
FOURIER_GROUPS = 4
FOURIER_GROUP_DIM = D_MODEL // FOURIER_GROUPS
MOE_GROUPS = 4
MOE_EXPERTS_PER_GROUP = 8
MOE_EXPERTS = MOE_GROUPS * MOE_EXPERTS_PER_GROUP
MOE_TOP_K = 2
MOE_HIDDEN = 256
RMS_EPS = 1e-6
LOG2_E = math.log2(math.e)

V7X_LANES = 128
V7X_VMEM_LIMIT = 48 * 1024 * 1024
TOKEN_TILE = 512
ATTN_Q_TILE = 256
DFT_ROW_TILE = 512
MOE_ROWS = 256
V7X_SUBLANES = 8
ROW_SUBLANES = D_MODEL // V7X_LANES
DFT_FINE = 64
DFT_REV_BLOCK = 128
DMA_UNROLL = 16
CONV_ROWS = 128

_F32 = jnp.float32
_BF16 = jnp.bfloat16
_NT = (((1,), (1,)), ((), ()))
_TN = (((0,), (0,)), ((), ()))


def _params(*sem):
    return pltpu.CompilerParams(dimension_semantics=sem, vmem_limit_bytes=V7X_VMEM_LIMIT)


def _tile(n, want):
    t = min(n, want)
    assert n % t == 0, (n, t)
    return t


def _full(shape):
    return pl.BlockSpec(shape, lambda *_: (0,) * len(shape))


def _rms_rows(x, w):
    ms = jnp.mean(x * x, axis=-1, keepdims=True)
    return x * lax.rsqrt(ms + RMS_EPS) * w


def _silu(x):
    return x * jax.nn.sigmoid(x)


def _softplus(x):
    return jnp.maximum(x, 0.0) + jnp.log1p(jnp.exp(-jnp.abs(x)))


def _in_proj_kernel(x_ref, nw_ref, wqkv_ref, wz_ref, wxbc_ref, wdt_ref, wdtt_ref, qn_ref, kn_ref,
                    q_ref, k_ref, v_ref, z_ref, xbc_ref, dt_ref, dtt_ref):
    tm = x_ref.shape[0]
    h = _rms_rows(x_ref[...], nw_ref[...]).astype(_BF16)
    qkv = jnp.dot(h, wqkv_ref[...], preferred_element_type=_F32)
    low = lax.broadcasted_iota(jnp.int32, (tm, V7X_LANES), 1) < DIFF_QK_DIM

    def qk_norm(slab, w):
        sq = slab * slab
        s_lo = jnp.sum(jnp.where(low, sq, 0.0), axis=-1, keepdims=True)
        s_hi = jnp.sum(jnp.where(low, 0.0, sq), axis=-1, keepdims=True)
        ms = jnp.where(low, s_lo, s_hi) * (1.0 / DIFF_QK_DIM)
        return slab * lax.rsqrt(ms + RMS_EPS) * w

    scale = DIFF_QK_DIM ** -0.5 * LOG2_E
    for hd in range(N_DIFF_HEADS):
        c0 = hd * DIFF_V_DIM
        q_ref[:, c0:c0 + DIFF_V_DIM] = (qk_norm(qkv[:, c0:c0 + DIFF_V_DIM], qn_ref[...]) * scale).astype(_BF16)
        k_ref[:, c0:c0 + DIFF_V_DIM] = qk_norm(qkv[:, Q_COLS + c0:Q_COLS + c0 + DIFF_V_DIM], kn_ref[...]).astype(_BF16)
    v_ref[...] = qkv[:, 2 * Q_COLS:].astype(_BF16)
    z_ref[...] = jnp.dot(h, wz_ref[...], preferred_element_type=_F32).astype(_BF16)
    xbc_ref[...] = jnp.dot(h, wxbc_ref[...], preferred_element_type=_F32)
    dt_ref[...] = jnp.dot(h, wdt_ref[...], preferred_element_type=_F32)
    dtt = lax.dot_general(wdtt_ref[...], h, _NT, preferred_element_type=_F32)
    for j in range(tm // SSD_CHUNK):
        dtt_ref[j] = dtt[:, j * SSD_CHUNK:(j + 1) * SSD_CHUNK]


def _in_proj(x2d, norm_w, w_in, q_norm, k_norm):
    t = x2d.shape[0]
    tm = _tile(t, TOKEN_TILE)
    c_qkv = 3 * Q_COLS
    c_z = c_qkv + SSD_WIDTH
    c_xbc = c_z + SSD_CONV_CH
    wb = w_in.astype(_BF16)
    wqkv, wz, wxbc, wdt = wb[:, :c_qkv], wb[:, c_qkv:c_z], wb[:, c_z:c_xbc], wb[:, c_xbc:]
    row = lambda i: (i, 0)
    outs = pl.pallas_call(
        _in_proj_kernel,
        grid=(t // tm,),
        in_specs=[pl.BlockSpec((tm, D_MODEL), row), _full((1, D_MODEL)),
                  _full((D_MODEL, c_qkv)), _full((D_MODEL, SSD_WIDTH)), _full((D_MODEL, SSD_CONV_CH)),
                  _full((D_MODEL, DT_COLS)), _full((DT_COLS, D_MODEL)),
                  _full((1, DIFF_V_DIM)), _full((1, DIFF_V_DIM))],
        out_specs=[pl.BlockSpec((tm, Q_COLS), row), pl.BlockSpec((tm, Q_COLS), row),
                   pl.BlockSpec((tm, ATTN_WIDTH), row), pl.BlockSpec((tm, SSD_WIDTH), row),
                   pl.BlockSpec((tm, SSD_CONV_CH), row), pl.BlockSpec((tm, DT_COLS), row),
                   pl.BlockSpec((tm // SSD_CHUNK, DT_COLS, SSD_CHUNK), lambda i: (i, 0, 0))],
        out_shape=[jax.ShapeDtypeStruct((t, Q_COLS), _BF16), jax.ShapeDtypeStruct((t, Q_COLS), _BF16),
                   jax.ShapeDtypeStruct((t, ATTN_WIDTH), _BF16), jax.ShapeDtypeStruct((t, SSD_WIDTH), _BF16),
                   jax.ShapeDtypeStruct((t, SSD_CONV_CH), _F32), jax.ShapeDtypeStruct((t, DT_COLS), _F32),
                   jax.ShapeDtypeStruct((t // SSD_CHUNK, DT_COLS, SSD_CHUNK), _F32)],
        compiler_params=_params("parallel"),
        name="in_proj",
    )(x2d, norm_w.reshape(1, D_MODEL), wqkv, wz, wxbc, wdt, wdt.T,
      jnp.tile(q_norm, 2).reshape(1, DIFF_V_DIM), jnp.tile(k_norm, 2).reshape(1, DIFF_V_DIM))
    return outs


def _attn_kernel(slope_ref, lq1_ref, lk1_ref, lq2_ref, lk2_ref, q_ref, k_ref, v_ref, subln_ref, o_ref,
                 bias_ref, *, lambda_init):
    tq = q_ref.shape[0]
    seq = k_ref.shape[0]
    hd = pl.program_id(0)
    qi = pl.program_id(1)

    @pl.when(pl.program_id(2) == 0)
    def _():
        row = qi * tq + lax.broadcasted_iota(jnp.int32, (tq, seq), 0)
        col = lax.broadcasted_iota(jnp.int32, (tq, seq), 1)
        bias_ref[...] = (slope_ref[hd] * LOG2_E) * jnp.abs(row - col).astype(_F32)

    lam = (jnp.exp(jnp.sum(lq1_ref[...] * lk1_ref[...], keepdims=True))
           - jnp.exp(jnp.sum(lq2_ref[...] * lk2_ref[...], keepdims=True)) + lambda_init)
    q = q_ref[...]
    k = k_ref[...]
    low = lax.broadcasted_iota(jnp.int32, (tq, V7X_LANES), 1) < DIFF_QK_DIM
    zero = jnp.zeros_like(q)

    def branch(qc):
        s = lax.dot_general(qc, k, _NT, preferred_element_type=_F32) - bias_ref[...]
        p = jnp.exp2(s - jnp.max(s, axis=-1, keepdims=True))
        return p, jnp.sum(p, axis=-1, keepdims=True)

    p1, l1 = branch(jnp.where(low, q, zero))
    p2, l2 = branch(jnp.where(low, zero, q))
    w = p1 * (1.0 / l1) - p2 * (lam / l2)
    o = jnp.dot(w.astype(_BF16), v_ref[...], preferred_element_type=_F32)
    o_ref[...] = (_rms_rows(o, subln_ref[...]) * (1.0 - lambda_init)).astype(_BF16)


def _diff_attention(q, k, v, lq1, lk1, lq2, lk2, subln, lambda_init, batch, seq):
    tq = _tile(seq, ATTN_Q_TILE)
    slopes = jnp.asarray([2.0 ** (-8.0 * (h + 1) / N_DIFF_HEADS) for h in range(N_DIFF_HEADS)], _F32)
    vec = lambda a: a.reshape(1, -1).astype(_F32)
    q3, k3, v3 = (a.reshape(batch, seq, ATTN_WIDTH) for a in (q, k, v))
    kv_spec = pl.BlockSpec((None, seq, DIFF_V_DIM), lambda h, i, b: (b, 0, h))
    q_spec = pl.BlockSpec((None, tq, DIFF_V_DIM), lambda h, i, b: (b, i, h))
    out = pl.pallas_call(
        functools.partial(_attn_kernel, lambda_init=lambda_init),
        grid=(N_DIFF_HEADS, seq // tq, batch),
        in_specs=[pl.BlockSpec(memory_space=pltpu.SMEM)] + [_full((1, DIFF_QK_DIM))] * 4
                 + [q_spec, kv_spec, kv_spec, _full((1, DIFF_V_DIM))],
        out_specs=q_spec,
        out_shape=jax.ShapeDtypeStruct((batch, seq, ATTN_WIDTH), _BF16),
        scratch_shapes=[pltpu.VMEM((tq, seq), _F32)],
        compiler_params=_params("parallel", "parallel", "arbitrary"),
        name="diff_attn",
    )(slopes, vec(lq1), vec(lk1), vec(lq2), vec(lk2), q3, k3, v3, vec(subln))
    return out.reshape(batch * seq, ATTN_WIDTH)


def _ssd_kernel(xbc_ref, z_ref, dt_ref, dtt_ref, cw_ref, cb_ref, dtb_row_ref, dtb_col_ref,
                alog_row_ref, alog_col_ref, dskip_ref, nw_ref, o_ref,
                xpad_ref, xc_ref, y_ref, state_ref):
    seq = xbc_ref.shape[0]
    n_chunks = seq // SSD_CHUNK
    L = SSD_CHUNK
    halo = 8
    pad = SSD_CONV // 2

    xpad_ref[0:halo, :] = jnp.zeros((halo, SSD_CONV_CH), _F32)
    xpad_ref[halo + seq:2 * halo + seq, :] = jnp.zeros((halo, SSD_CONV_CH), _F32)
    xpad_ref[halo:halo + seq, :] = xbc_ref[...]
    for rb in range(seq // CONV_ROWS):
        base = rb * CONV_ROWS + halo - pad
        acc = jnp.broadcast_to(cb_ref[...], (CONV_ROWS, SSD_CONV_CH))
        for j in range(SSD_CONV):
            acc = acc + cw_ref[j:j + 1, :] * xpad_ref[base + j:base + j + CONV_ROWS, :]
        xc_ref[rb * CONV_ROWS:(rb + 1) * CONV_ROWS, :] = _silu(acc).astype(_BF16)

    sub = lax.broadcasted_iota(jnp.int32, (L, L), 0)
    lane = lax.broadcasted_iota(jnp.int32, (L, L), 1)
    causal = lane <= sub
    anti = lane >= sub
    tri = causal.astype(_BF16)
    trit = anti.astype(_BF16)
    neg_big = jnp.float32(-1e30)
    head_of_col = lax.broadcasted_iota(jnp.int32, (DT_COLS, SSD_WIDTH), 1) // SSD_HEAD_DIM
    dtcol = lax.broadcasted_iota(jnp.int32, (DT_COLS, SSD_WIDTH), 0)
    expand_f = (dtcol == head_of_col).astype(_BF16)
    expand_b = (dtcol == head_of_col + SSD_HEADS).astype(_BF16)
    state_row_group = lax.broadcasted_iota(jnp.int32, (SSD_BC, SSD_WIDTH), 0) // SSD_STATE
    state_col_group = (lax.broadcasted_iota(jnp.int32, (SSD_BC, SSD_WIDTH), 1)
                       // (SSD_HEAD_DIM * SSD_HEADS_PER_GROUP))
    state_mask = state_row_group == state_col_group
    a_row = -jnp.exp(alog_row_ref[...])
    a_col = -jnp.exp(alog_col_ref[...])

    def split(a, n):
        parts = []
        for _ in range(n - 1):
            p = a.astype(_BF16)
            parts.append(p)
            a = a - p.astype(_F32)
        return parts + [a.astype(_BF16)]

    def cumsum_cols(a):
        return sum(jnp.dot(tri, p, preferred_element_type=_F32) for p in split(a, 3))

    def cumsum_rows(a):
        return sum(jnp.dot(p, trit, preferred_element_type=_F32) for p in split(a, 3))

    def spread(v, expand):
        return sum(jnp.dot(p, expand, preferred_element_type=_F32) for p in split(v, 2))

    def load_chunk(c):
        r0 = pl.multiple_of(c * L, L)
        xs = xc_ref[pl.ds(r0, L), 0:SSD_WIDTH]
        bc = xc_ref[pl.ds(r0, L), SSD_WIDTH:SSD_WIDTH + SSD_BC]
        cc = xc_ref[pl.ds(r0, L), SSD_WIDTH + SSD_BC:SSD_CONV_CH]
        dtc = _softplus(dt_ref[pl.ds(r0, L), :] + dtb_row_ref[...])
        a_c = dtc * a_row
        acs_c = cumsum_cols(a_c)
        return r0, xs, bc, cc, dtc, a_c, acs_c

    def state_step(state, bc, xs, weight_cols, total_row, expand):
        wexp = spread(weight_cols, expand)
        xw = (xs.astype(_F32) * wexp).astype(_BF16)
        new = lax.dot_general(bc, xw, _TN, preferred_element_type=_F32)
        decay = jnp.exp(spread(jnp.broadcast_to(total_row, (8, DT_COLS)), expand))[0:1, :]
        return state * decay + jnp.where(state_mask, new, 0.0)

    state_ref[...] = jnp.zeros_like(state_ref)

    def fwd_body(c, carry):
        r0, xs, bc, cc, dtc, a_c, acs_c = load_chunk(c)
        dtr = _softplus(dtt_ref[c] + dtb_col_ref[...])
        a_r = dtr * a_col
        acs_r = cumsum_rows(a_r)
        ecs_c = acs_c - a_c
        ecs_r = acs_r - a_r
        lane_bc = lax.broadcasted_iota(jnp.int32, (L, SSD_BC), 1)
        gmat = []
        for g in range(SSD_GROUPS):
            in_g = (lane_bc >= g * SSD_STATE) & (lane_bc < (g + 1) * SSD_STATE)
            cg = jnp.where(in_g, cc, jnp.zeros_like(cc))
            gmat.append(lax.dot_general(cg, bc, _NT, preferred_element_type=_F32))
        parts = []
        for h in range(SSD_HEADS):
            hb = SSD_HEADS + h
            lf = jnp.exp(jnp.where(causal, acs_c[:, h:h + 1] - acs_r[h:h + 1, :], neg_big))
            lb = jnp.exp(jnp.where(anti, ecs_r[hb:hb + 1, :] - ecs_c[:, hb:hb + 1], neg_big))
            w = gmat[h // SSD_HEADS_PER_GROUP] * (lf * dtr[h:h + 1, :] + lb * dtr[hb:hb + 1, :])
            parts.append(jnp.dot(w.astype(_BF16), xs[:, h * SSD_HEAD_DIM:(h + 1) * SSD_HEAD_DIM],
                                 preferred_element_type=_F32))
        y_diag = jnp.concatenate(parts, axis=-1)
        state = state_ref[...]
        y_off = (jnp.dot(cc, state.astype(_BF16), preferred_element_type=_F32)
                 * spread(jnp.exp(acs_c), expand_f))
        y_ref[pl.ds(r0, L), :] = y_diag + y_off
        total = acs_c[L - 1:L, :]
        state_ref[...] = state_step(state, bc, xs, jnp.exp(total - acs_c) * dtc, total, expand_f)
        return carry

    lax.fori_loop(0, n_chunks, fwd_body, 0, unroll=4)

    state_ref[...] = jnp.zeros_like(state_ref)

    def bwd_body(i, carry):
        c = n_chunks - 1 - i
        r0, xs, bc, cc, dtc, a_c, acs_c = load_chunk(c)
        ecs_c = acs_c - a_c
        total = acs_c[L - 1:L, :]
        state = state_ref[...]
        y_off = (jnp.dot(cc, state.astype(_BF16), preferred_element_type=_F32)
                 * spread(jnp.exp(total - ecs_c), expand_b))
        y = y_ref[pl.ds(r0, L), :] + y_off + dskip_ref[...] * xs.astype(_F32)
        gated = y * _silu(z_ref[pl.ds(r0, L), :].astype(_F32))
        o_ref[pl.ds(r0, L), :] = _rms_rows(gated, nw_ref[...]).astype(_BF16)
        state_ref[...] = state_step(state, bc, xs, jnp.exp(ecs_c) * dtc, total, expand_b)
        return carry

    lax.fori_loop(0, n_chunks, bwd_body, 0, unroll=4)


def _ssd_mixer(z, xbc, dt, dtt, conv_w, conv_b, dt_bias, a_log, d_skip, norm_w, batch, seq):
    n_chunks = seq // SSD_CHUNK
    per_b = lambda w: pl.BlockSpec((None, seq, w), lambda b: (b, 0, 0))
    dtb = dt_bias.reshape(DT_COLS).astype(_F32)
    alog = a_log.reshape(DT_COLS).astype(_F32)
    out = pl.pallas_call(
        _ssd_kernel,
        grid=(batch,),
        in_specs=[per_b(SSD_CONV_CH), per_b(SSD_WIDTH), per_b(DT_COLS),
                  pl.BlockSpec((n_chunks, DT_COLS, SSD_CHUNK), lambda b: (b, 0, 0)),
                  _full((SSD_CONV, SSD_CONV_CH)), _full((1, SSD_CONV_CH)),
                  _full((1, DT_COLS)), _full((DT_COLS, 1)), _full((1, DT_COLS)), _full((DT_COLS, 1)),
                  _full((1, SSD_WIDTH)), _full((1, SSD_WIDTH))],
        out_specs=per_b(SSD_WIDTH),
        out_shape=jax.ShapeDtypeStruct((batch, seq, SSD_WIDTH), _BF16),
        scratch_shapes=[pltpu.VMEM((seq + 16, SSD_CONV_CH), _F32), pltpu.VMEM((seq, SSD_CONV_CH), _BF16),
                        pltpu.VMEM((seq, SSD_WIDTH), _F32), pltpu.VMEM((SSD_BC, SSD_WIDTH), _F32)],
        compiler_params=_params("parallel"),
        name="ssd_mixer",
    )(xbc.reshape(batch, seq, SSD_CONV_CH), z.reshape(batch, seq, SSD_WIDTH), dt.reshape(batch, seq, DT_COLS),
      dtt, conv_w.astype(_F32), conv_b.reshape(1, SSD_CONV_CH).astype(_F32),
      dtb.reshape(1, DT_COLS), dtb.reshape(DT_COLS, 1), alog.reshape(1, DT_COLS), alog.reshape(DT_COLS, 1),
      jnp.repeat(d_skip.astype(_F32), SSD_HEAD_DIM).reshape(1, SSD_WIDTH), norm_w.reshape(1, SSD_WIDTH))
    return out.reshape(batch * seq, SSD_WIDTH)


def _proj_router_kernel(*refs, n_act):
    x_ref = refs[0]
    acts = refs[1:1 + n_act]
    ws = refs[1 + n_act:1 + 2 * n_act]
    nw_ref, wr_hi_ref, wr_lo_ref, br_ref, x1_ref, h_ref, route_ref, gate_ref, count_ref = refs[1 + 2 * n_act:]
    tm = x_ref.shape[0]
    x1 = x_ref[...]
    for a_ref, w_ref in zip(acts, ws):
        x1 = x1 + jnp.dot(a_ref[...], w_ref[...], preferred_element_type=_F32)
    x1_ref[...] = x1
    h = _rms_rows(x1, nw_ref[...])
    _store_row_tiles(h_ref, h)

    h_hi = h.astype(_BF16)
    h_lo = (h - h_hi.astype(_F32)).astype(_BF16)
    logits = (jnp.dot(h_hi, wr_hi_ref[...], preferred_element_type=_F32)
              + jnp.dot(h_lo, wr_hi_ref[...], preferred_element_type=_F32)
              + jnp.dot(h_hi, wr_lo_ref[...], preferred_element_type=_F32) + br_ref[...])
    lane = lax.broadcasted_iota(jnp.int32, (tm, V7X_LANES), 1)
    lane_f = lane.astype(_F32)
    ninf = jnp.float32(-jnp.inf)
    nolane = jnp.float32(V7X_LANES)

    def argmax_first(vals):
        m = jnp.max(vals, axis=-1, keepdims=True)
        idx = jnp.min(jnp.where(vals == m, lane_f, nolane), axis=-1, keepdims=True)
        return m, idx

    gl = jnp.where(lane < MOE_GROUPS, logits, ninf)
    gmax, gidx = argmax_first(gl)
    g_p = 1.0 / jnp.sum(jnp.exp(gl - gmax), axis=-1, keepdims=True)
    first = MOE_GROUPS + MOE_EXPERTS_PER_GROUP * gidx
    el = jnp.where((lane_f >= first) & (lane_f < first + MOE_EXPERTS_PER_GROUP), logits, ninf)
    e1, i1 = argmax_first(el)
    e2, i2 = argmax_first(jnp.where(lane_f == i1, ninf, el))
    r = jnp.exp(e2 - e1)
    p1 = 1.0 / (1.0 + r)
    gate_ref[...] = jnp.where(lane == 0, g_p * p1, jnp.where(lane == 1, g_p * (r * p1), 0.0))

    @pl.when(pl.program_id(0) == 0)
    def _():
        count_ref[...] = jnp.zeros_like(count_ref)

    chosen = (lane_f == i1) | (lane_f == i2)
    onehot = jnp.where(chosen, 1.0, 0.0)
    earlier = (lax.broadcasted_iota(jnp.int32, (tm, tm), 1)
               < lax.broadcasted_iota(jnp.int32, (tm, tm), 0)).astype(_BF16)
    prior = jnp.dot(earlier, onehot.astype(_BF16), preferred_element_type=_F32) + count_ref[0:1, :]
    rank1 = jnp.sum(jnp.where(lane_f == i1, prior, 0.0), axis=-1, keepdims=True)
    rank2 = jnp.sum(jnp.where(lane_f == i2, prior, 0.0), axis=-1, keepdims=True)
    count_ref[...] = count_ref[...] + jnp.sum(onehot, axis=0, keepdims=True)
    route = jnp.where(lane == 0, i1 - MOE_GROUPS, jnp.where(lane == 1, i2 - MOE_GROUPS,
                      jnp.where(lane == 2, rank1, jnp.where(lane == 3, rank2, 0.0))))
    route_ref[...] = route.astype(jnp.int32)


def _proj_router(x2d, acts, weights, norm_w, rg_w, rg_b, re_w, re_b):
    t = x2d.shape[0]
    tm = _tile(t, TOKEN_TILE)
    n_act = len(acts)
    n_log = MOE_GROUPS + MOE_EXPERTS
    wr = jnp.zeros((D_MODEL, V7X_LANES), _F32).at[:, :n_log].set(
        jnp.concatenate([rg_w.astype(_F32), re_w.reshape(D_MODEL, MOE_EXPERTS).astype(_F32)], axis=1))
    br = jnp.zeros((1, V7X_LANES), _F32).at[0, :n_log].set(
        jnp.concatenate([rg_b.astype(_F32), re_b.reshape(MOE_EXPERTS).astype(_F32)]))
    wr_hi = wr.astype(_BF16)
    wr_lo = (wr - wr_hi.astype(_F32)).astype(_BF16)
    row = lambda i: (i, 0)
    x1, h, route, gate, counts = pl.pallas_call(
        functools.partial(_proj_router_kernel, n_act=n_act),
        grid=(t // tm,),
        in_specs=[pl.BlockSpec((tm, D_MODEL), row)]
                 + [pl.BlockSpec((tm, a.shape[1]), row) for a in acts]
                 + [_full(w.shape) for w in weights]
                 + [_full((1, D_MODEL)), _full((D_MODEL, V7X_LANES)), _full((D_MODEL, V7X_LANES)),
                    _full((1, V7X_LANES))],
        out_specs=[pl.BlockSpec((tm, D_MODEL), row), pl.BlockSpec((tm * ROW_SUBLANES, V7X_LANES), row),
                   pl.BlockSpec((tm, V7X_LANES), row), pl.BlockSpec((tm, V7X_LANES), row),
                   _full((V7X_SUBLANES, V7X_LANES))],
        out_shape=[jax.ShapeDtypeStruct((t, D_MODEL), _F32),
                   jax.ShapeDtypeStruct((t * ROW_SUBLANES, V7X_LANES), _F32),
                   jax.ShapeDtypeStruct((t, V7X_LANES), jnp.int32), jax.ShapeDtypeStruct((t, V7X_LANES), _F32),
                   jax.ShapeDtypeStruct((V7X_SUBLANES, V7X_LANES), _F32)],
        compiler_params=_params("arbitrary"),
        name="proj_router",
    )(x2d, *acts, *[w.astype(_BF16) for w in weights], norm_w.reshape(1, D_MODEL), wr_hi, wr_lo, br)
    counts = counts[0, MOE_GROUPS:MOE_GROUPS + MOE_EXPERTS].astype(jnp.int32)
    return x1, h, route, gate, counts


def _store_row_tiles(ref, val):
    n = val.shape[0]
    for s in range(ROW_SUBLANES):
        ref[pl.ds(s, n, stride=ROW_SUBLANES), :] = val[:, s * V7X_LANES:(s + 1) * V7X_LANES]


def _load_row_tiles(ref, n):
    return jnp.concatenate([ref[pl.ds(s, n, stride=ROW_SUBLANES), :] for s in range(ROW_SUBLANES)], axis=1)


def _expert_kernel(bexp_ref, nused_ref, gidx_ref, gnext_ref, sidx_ref, h_hbm, w1_ref, w3_ref, w2_ref, out_hbm,
                   xbuf, ybuf, w13_bf, w2_bf, sem_in, sem_out):
    i = pl.program_id(0)

    @pl.when((i == 0) | (bexp_ref[i] != bexp_ref[jnp.maximum(i - 1, 0)]))
    def _():
        w13_bf[:, :MOE_HIDDEN] = w1_ref[...].astype(_BF16)
        w13_bf[:, MOE_HIDDEN:] = w3_ref[...].astype(_BF16)
        w2_bf[...] = w2_ref[...].astype(_BF16)

    n_used = nused_ref[0]
    slot = lax.rem(i, 2)
    other = 1 - slot
    tile = ROW_SUBLANES

    def gather_copy(buf_slot, r, tok):
        return pltpu.make_async_copy(h_hbm.at[pl.ds(pl.multiple_of(tok * tile, tile), tile)],
                                     xbuf.at[buf_slot, pl.ds(pl.multiple_of(r * tile, tile), tile)],
                                     sem_in.at[buf_slot])

    def scatter_copy(buf_slot, r, row):
        return pltpu.make_async_copy(ybuf.at[buf_slot, pl.ds(pl.multiple_of(r * tile, tile), tile)],
                                     out_hbm.at[pl.ds(pl.multiple_of(row * tile, tile), tile)],
                                     sem_out.at[buf_slot])

    def for_rows(fn):
        def body(j, carry):
            for u in range(DMA_UNROLL):
                fn(j * DMA_UNROLL + u, u % 2)
            return carry
        lax.fori_loop(0, MOE_ROWS // DMA_UNROLL, body, 0)

    def start_gather(idx_ref, buf_slot):
        for_rows(lambda r, queue: gather_copy(buf_slot, r, idx_ref[0, r]).start(priority=queue))

    def wait_gather(buf_slot):
        for r in range(MOE_ROWS):
            gather_copy(buf_slot, r, 0).wait()

    def start_scatter(buf_slot):
        for_rows(lambda r, queue: scatter_copy(buf_slot, r, sidx_ref[0, r]).start(priority=queue))

    def wait_scatter(buf_slot):
        for r in range(MOE_ROWS):
            scatter_copy(buf_slot, r, 0).wait()

    @pl.when(i == 0)
    def _():
        spare0 = out_hbm.shape[0] - 2 * MOE_ROWS * tile
        ybuf[...] = jnp.zeros_like(ybuf)
        fills = [pltpu.make_async_copy(ybuf.at[s], out_hbm.at[pl.ds(spare0 + s * MOE_ROWS * tile, MOE_ROWS * tile)],
                                       sem_out.at[s]) for s in range(2)]
        for f in fills:
            f.start()
        for f in fills:
            f.wait()
        start_gather(gidx_ref, 0)

    @pl.when(i + 1 < n_used)
    def _():
        start_gather(gnext_ref, other)

    @pl.when(i < n_used)
    def _():
        wait_gather(slot)
        xb = _load_row_tiles(xbuf.at[slot], MOE_ROWS).astype(_BF16)
        a = jnp.dot(xb, w13_bf[...], preferred_element_type=_F32)
        hid = _silu(a[:, :MOE_HIDDEN]) * a[:, MOE_HIDDEN:]
        y = jnp.dot(hid.astype(_BF16), w2_bf[...], preferred_element_type=_F32)

        @pl.when(i >= 2)
        def _():
            wait_scatter(slot)

        _store_row_tiles(ybuf.at[slot], y)
        start_scatter(slot)

        @pl.when(i == n_used - 1)
        def _():
            wait_scatter(slot)

            @pl.when(i >= 1)
            def _():
                wait_scatter(other)


def _expert_plan(route, counts, n_tok):
    n_assign = n_tok * MOE_TOP_K
    n_blocks = n_assign // MOE_ROWS + MOE_EXPERTS
    n_slots = n_blocks * MOE_ROWS
    experts = jnp.arange(MOE_EXPERTS, dtype=jnp.int32)
    nblk = (counts + MOE_ROWS - 1) // MOE_ROWS
    blk_end = jnp.cumsum(nblk)
    blk_first = blk_end - nblk
    n_used = blk_end[-1]
    bid = jnp.arange(n_blocks, dtype=jnp.int32)
    bexp = jnp.minimum(jnp.sum(bid[:, None] >= blk_end[None, :], axis=1), MOE_EXPERTS - 1).astype(jnp.int32)
    last_exp = jnp.max(jnp.where(counts > 0, experts, 0))
    bexp = jnp.where(bid < n_used, bexp, last_exp)
    e = route[:, 0:MOE_TOP_K]
    rank = route[:, MOE_TOP_K:2 * MOE_TOP_K]
    first_slot = jnp.sum(jnp.where(e[:, :, None] == experts, blk_first * MOE_ROWS, 0), axis=-1)
    dest = (first_slot + rank).reshape(n_assign)
    tok = jnp.broadcast_to(jnp.arange(n_tok, dtype=jnp.int32)[:, None], (n_tok, MOE_TOP_K))
    row = tok + jnp.arange(MOE_TOP_K, dtype=jnp.int32) * n_tok
    r = jnp.arange(MOE_ROWS, dtype=jnp.int32)
    spare = (MOE_TOP_K * n_tok + (bid[:, None] % 2) * MOE_ROWS + r[None, :]).reshape(n_slots)
    spare_at_dest = MOE_TOP_K * n_tok + ((dest // MOE_ROWS) % 2) * MOE_ROWS + dest % MOE_ROWS
    gidx = jnp.zeros((n_slots,), jnp.int32).at[dest].add(tok.reshape(n_assign))
    sidx = spare + jnp.zeros((n_slots,), jnp.int32).at[dest].add(row.reshape(n_assign) - spare_at_dest)
    shape3 = (n_blocks, 1, MOE_ROWS)
    return bexp, n_used.reshape(1).astype(jnp.int32), gidx.reshape(shape3), sidx.reshape(shape3), n_blocks


def _experts(h_tiles, route, counts, w1, w3, w2, layer):
    n_tok = h_tiles.shape[0] // ROW_SUBLANES
    bexp, n_used, gidx, sidx, n_blocks = _expert_plan(route, counts, n_tok)
    idx_block = (None, 1, MOE_ROWS)
    per_expert = lambda i, bexp, nu: (layer, bexp[i], 0, 0)
    grid_spec = pltpu.PrefetchScalarGridSpec(
        num_scalar_prefetch=2,
        grid=(n_blocks,),
        in_specs=[pl.BlockSpec(idx_block, lambda i, bexp, nu: (i, 0, 0), memory_space=pltpu.SMEM),
                  pl.BlockSpec(idx_block, lambda i, bexp, nu: (jnp.minimum(i + 1, n_blocks - 1), 0, 0),
                               memory_space=pltpu.SMEM),
                  pl.BlockSpec(idx_block, lambda i, bexp, nu: (i, 0, 0), memory_space=pltpu.SMEM),
                  pl.BlockSpec(memory_space=pl.ANY),
                  pl.BlockSpec((None, None, D_MODEL, MOE_HIDDEN), per_expert),
                  pl.BlockSpec((None, None, D_MODEL, MOE_HIDDEN), per_expert),
                  pl.BlockSpec((None, None, MOE_HIDDEN, D_MODEL), per_expert)],
        out_specs=pl.BlockSpec(memory_space=pl.ANY),
        scratch_shapes=[pltpu.VMEM((2, MOE_ROWS * ROW_SUBLANES, V7X_LANES), _F32),
                        pltpu.VMEM((2, MOE_ROWS * ROW_SUBLANES, V7X_LANES), _F32),
                        pltpu.VMEM((D_MODEL, 2 * MOE_HIDDEN), _BF16), pltpu.VMEM((MOE_HIDDEN, D_MODEL), _BF16),
                        pltpu.SemaphoreType.DMA((2,)), pltpu.SemaphoreType.DMA((2,))],
    )
    out_rows = MOE_TOP_K * n_tok + 2 * MOE_ROWS
    return pl.pallas_call(
        _expert_kernel,
        grid_spec=grid_spec,
        out_shape=jax.ShapeDtypeStruct((out_rows * ROW_SUBLANES, V7X_LANES), _F32),
        compiler_params=_params("arbitrary"),
        name="experts",
    )(bexp, n_used, gidx, gidx, sidx, h_tiles, w1, w3, w2)


def _combine(x1_ref, y0_ref, y1_ref, gate_ref):
    gate = gate_ref[...]
    tm = x1_ref.shape[0]
    return x1_ref[...] + gate[:, 0:1] * _load_row_tiles(y0_ref, tm) + gate[:, 1:2] * _load_row_tiles(y1_ref, tm)


def _combine_kernel(x1_ref, y0_ref, y1_ref, gate_ref, o_ref):
    o_ref[...] = _combine(x1_ref, y0_ref, y1_ref, gate_ref)


def _combine_dft_kernel(x1_ref, y0_ref, y1_ref, gate_ref, nw_ref, cosc_ref, sinc_ref, x_ref, hc_ref, hs_ref):
    x = _combine(x1_ref, y0_ref, y1_ref, gate_ref)
    x_ref[...] = x
    h = _rms_rows(x, nw_ref[...]).astype(_BF16)
    for g in range(FOURIER_GROUPS):
        c0 = g * FOURIER_GROUP_DIM
        hg = h[:, c0:c0 + FOURIER_GROUP_DIM]
        hc_ref[:, c0:c0 + FOURIER_GROUP_DIM] = jnp.dot(hg, cosc_ref[...], preferred_element_type=_F32).astype(_BF16)
        hs_ref[:, c0:c0 + FOURIER_GROUP_DIM] = jnp.dot(hg, sinc_ref[...], preferred_element_type=_F32).astype(_BF16)


def _moe_combine(x1, y, gate, dft=None):
    t = x1.shape[0]
    tm = _tile(t, TOKEN_TILE)
    row = lambda i: (i, 0)
    second = lambda i: (i + t // tm, 0)
    y_block = (tm * ROW_SUBLANES, V7X_LANES)
    in_specs = [pl.BlockSpec((tm, D_MODEL), row), pl.BlockSpec(y_block, row),
                pl.BlockSpec(y_block, second), pl.BlockSpec((tm, V7X_LANES), row)]
    if dft is None:
        return pl.pallas_call(
            _combine_kernel, grid=(t // tm,), in_specs=in_specs,
            out_specs=pl.BlockSpec((tm, D_MODEL), row),
            out_shape=jax.ShapeDtypeStruct((t, D_MODEL), _F32),
            compiler_params=_params("parallel"), name="moe_combine",
        )(x1, y, y, gate)
    norm_w, cosc, sinc = dft
    return pl.pallas_call(
        _combine_dft_kernel, grid=(t // tm,),
        in_specs=in_specs + [_full((1, D_MODEL)), _full(cosc.shape), _full(sinc.shape)],
        out_specs=[pl.BlockSpec((tm, D_MODEL), row)] * 3,
        out_shape=[jax.ShapeDtypeStruct((t, D_MODEL), _F32), jax.ShapeDtypeStruct((t, D_MODEL), _BF16),
                   jax.ShapeDtypeStruct((t, D_MODEL), _BF16)],
        compiler_params=_params("parallel"), name="moe_combine_dft",
    )(x1, y, y, gate, norm_w.reshape(1, D_MODEL), cosc, sinc)


def _seq_dft_kernel(ca_ref, sa_ref, cb_ref, sb_ref, hc_ref, hs_ref, o_ref, cos_tab, nsin_tab, ec_ref, os_ref,
                    b_ref, *, scale):
    seq = hc_ref.shape[0]
    half = seq // 2
    blk = DFT_REV_BLOCK
    n_blk = half // blk

    @pl.when(pl.program_id(0) == 0)
    def _():
        cb = cb_ref[...]
        sb = sb_ref[...]

        def gen(j1, carry):
            r0 = pl.multiple_of(j1 * DFT_FINE, DFT_FINE)
            ca = ca_ref[pl.ds(j1, 1), :]
            sa = sa_ref[pl.ds(j1, 1), :]
            cos_tab[pl.ds(r0, DFT_FINE), :] = (ca * cb - sa * sb).astype(_BF16)
            nsin_tab[pl.ds(r0, DFT_FINE), :] = (-(sa * cb + ca * sb)).astype(_BF16)
            return carry
        lax.fori_loop(0, cos_tab.shape[0] // DFT_FINE, gen, 0)

    ri = lax.broadcasted_iota(jnp.int32, (blk, blk), 0)
    ci = lax.broadcasted_iota(jnp.int32, (blk, blk), 1)
    flip = (ci == blk - ri).astype(_BF16)
    row0 = lax.broadcasted_iota(jnp.int32, (blk, 1), 0) == 0

    def mirrored(ref, kb):
        lo = half + (n_blk - 1 - kb) * blk
        m = jnp.dot(flip, ref[lo:lo + blk, :], preferred_element_type=_F32)
        if kb >= 1:
            m = jnp.where(row0, ref[lo + blk:lo + blk + 1, :].astype(_F32), m)
        return m

    for kb in range(n_blk):
        rows = slice(kb * blk, (kb + 1) * blk)
        ec_ref[rows, :] = (hc_ref[rows, :].astype(_F32) + mirrored(hc_ref, kb)).astype(_BF16)
        os_ref[rows, :] = (hs_ref[rows, :].astype(_F32) - mirrored(hs_ref, kb)).astype(_BF16)

    mid = hc_ref[half:half + 1, :].astype(_F32) * scale
    tq = min(half, DFT_ROW_TILE)

    def rows_out(t, carry):
        r0 = pl.multiple_of(t * tq, tq)
        j = r0 + lax.broadcasted_iota(jnp.int32, (tq, 1), 0)
        base = (jnp.dot(cos_tab[pl.ds(r0, tq), :], ec_ref[...], preferred_element_type=_F32)
                + (1 - 2 * (j & 1)).astype(_F32) * mid)
        nq = jnp.dot(nsin_tab[pl.ds(r0, tq), :], os_ref[...], preferred_element_type=_F32)
        o_ref[pl.ds(r0, tq), :] = (base + nq).astype(_BF16)
        b_ref[pl.ds(r0, tq), :] = (base - nq).astype(_BF16)
        return carry
    lax.fori_loop(0, half // tq, rows_out, 0)

    sign_half = 1.0 - 2.0 * (half % 2)
    row_half = (jnp.dot(cos_tab[half:half + V7X_SUBLANES, :], ec_ref[...], preferred_element_type=_F32)[0:1, :]
                + sign_half * mid)
    for db in range(n_blk):
        src = (n_blk - 1 - db) * blk
        m = jnp.dot(flip, b_ref[src:src + blk, :], preferred_element_type=_F32)
        first = b_ref[src + blk:src + blk + 1, :].astype(_F32) if db >= 1 else row_half
        o_ref[half + db * blk:half + (db + 1) * blk, :] = jnp.where(row0, first, m).astype(_BF16)


def _seq_dft(hc, hs, batch, seq):
    half = seq // 2
    assert seq % DFT_FINE == 0 and half % DFT_REV_BLOCK == 0
    tab_rows = half + DFT_FINE
    n_coarse = tab_rows // DFT_FINE
    scale = seq ** -0.5
    ca, sa = _dft_tables(jnp.arange(n_coarse, dtype=jnp.int32) * DFT_FINE, seq, scale, half)
    cb, sb = _dft_tables(jnp.arange(DFT_FINE, dtype=jnp.int32), seq, 1.0, half)
    per_b = pl.BlockSpec((None, seq, D_MODEL), lambda b: (b, 0, 0))
    out = pl.pallas_call(
        functools.partial(_seq_dft_kernel, scale=scale), grid=(batch,),
        in_specs=[_full((n_coarse, half)), _full((n_coarse, half)), _full((DFT_FINE, half)), _full((DFT_FINE, half)),
                  per_b, per_b],
        out_specs=per_b,
        out_shape=jax.ShapeDtypeStruct((batch, seq, D_MODEL), _BF16),
        scratch_shapes=[pltpu.VMEM((tab_rows, half), _BF16), pltpu.VMEM((tab_rows, half), _BF16),
                        pltpu.VMEM((half, D_MODEL), _BF16), pltpu.VMEM((half, D_MODEL), _BF16),
                        pltpu.VMEM((half, D_MODEL), _BF16)],
        compiler_params=_params("arbitrary"), name="seq_dft",
    )(ca, sa, cb, sb, hc.reshape(batch, seq, D_MODEL), hs.reshape(batch, seq, D_MODEL))
    return out.reshape(batch * seq, D_MODEL)


def _dft_tables(rows, n, scale, n_cols):
    k = jnp.arange(n_cols, dtype=jnp.int32)
    m = (rows[:, None] * k[None, :]) % n
    ang = m.astype(_F32) * (2.0 * math.pi / n)
    return jnp.cos(ang) * scale, jnp.sin(ang) * scale


def kernel(x, norm_mix, norm_ffn, w_in, q_norm, k_norm, lambda_q1, lambda_k1, lambda_q2, lambda_k2,
           attn_subln, conv_w, conv_b, dt_bias, a_log, d_skip, ssd_norm, w_out, w_fourier,
           router_group_w, router_group_b, router_expert_w, router_expert_b,
           expert_w1, expert_w3, expert_w2):
    batch, seq, _ = x.shape
    depth = norm_mix.shape[0]
    xt = x.reshape(batch * seq, D_MODEL)
    pending = None
    for l in range(depth):
        i = l // 2
        if l % 2 == 0:
            if pending is not None:
                xt = _moe_combine(*pending)
            lambda_init = 0.8 - 0.6 * math.exp(-0.3 * l)
            q, k, v, z, xbc, dt, dtt = _in_proj(xt, norm_mix[l], w_in[i], q_norm[i], k_norm[i])
            attn = _diff_attention(q, k, v, lambda_q1[i], lambda_k1[i], lambda_q2[i], lambda_k2[i],
                                   attn_subln[i], lambda_init, batch, seq)
            ssd = _ssd_mixer(z, xbc, dt, dtt, conv_w[i], conv_b[i], dt_bias[i], a_log[i], d_skip[i],
                             ssd_norm[i], batch, seq)
            acts = [attn, ssd]
            weights = [w_out[i][:ATTN_WIDTH], w_out[i][ATTN_WIDTH:]]
        else:
            cos_c, sin_c = _dft_tables(jnp.arange(FOURIER_GROUP_DIM, dtype=jnp.int32), FOURIER_GROUP_DIM,
                                       FOURIER_GROUP_DIM ** -0.5, FOURIER_GROUP_DIM)
            dft = (norm_mix[l], cos_c.astype(_BF16), sin_c.astype(_BF16))
            if pending is None:
                zeros = jnp.zeros((2 * batch * seq * ROW_SUBLANES, V7X_LANES), _F32)
                pending = (xt, zeros, jnp.zeros((batch * seq, V7X_LANES), _F32))
            xt, hc, hs = _moe_combine(*pending, dft=dft)
            f = _seq_dft(hc, hs, batch, seq)
            acts = [f]
            weights = [w_fourier[i]]
        x1, h, route, gate, counts = _proj_router(xt, acts, weights, norm_ffn[l], router_group_w[l],
                                                  router_group_b[l], router_expert_w[l], router_expert_b[l])
        y = _experts(h, route, counts, expert_w1, expert_w3, expert_w2, l)
        pending = (x1, y, gate)
    xt = _moe_combine(*pending)
    return xt.reshape(batch, seq, D_MODEL)
```
